```python
import jax, jax.numpy as jnp
from jax import lax
import numpy as np

D_MODEL = 2048
BATCH = 4
SEQ = 2048
DEPTH = 2
DEC_BATCH = 8
DEC_SEQ = 1
PAST_LEN = 16384
PAGE_SIZE = 128

N_MIXERS = 2
N_SB_LAYERS = (DEPTH + 1) // 2
N_RET_LAYERS = DEPTH // 2
SB_HEADS = 16
SB_HEAD_DIM = D_MODEL // SB_HEADS
SB_BIAS_INIT = -6.0
Q_BLOCK = 128
RET_HEADS = 8
RET_QK_DIM = D_MODEL // RET_HEADS
RET_V_DIM = 2 * D_MODEL // RET_HEADS
RET_CHUNK = 128
ROPE_BASE = 10000.0
D_FF = 5632
EPS = 1e-6
N_NORMS = 6

kernel_name = "stickbreak_retention_macaron_decode_step"

F32 = jnp.float32


def rmsnorm(x, g):
    x32 = x.astype(F32)
    y = x32 * lax.rsqrt(jnp.mean(x32 * x32, axis=-1, keepdims=True) + EPS)
    return (y * g.astype(F32)).astype(x.dtype)


def swiglu(h, w_gate, w_up, w_down):
    return (jax.nn.silu(h @ w_gate) * (h @ w_up)) @ w_down


def half_ffn(x, g_pre, g_post, w_gate, w_up, w_down):
    return x + 0.5 * rmsnorm(swiglu(rmsnorm(x, g_pre), w_gate, w_up, w_down), g_post)


def sb_project(h, w_qkv):
    b, s, _ = h.shape
    qkv = (h @ w_qkv).reshape(b, s, 3, SB_HEADS, SB_HEAD_DIM)
    return qkv[:, :, 0], qkv[:, :, 1], qkv[:, :, 2]


def sb_attend(q, k, v, pos_q, pos_k, bias):
    z = (jnp.einsum('bqhd,bkhd->bhqk', q.astype(F32), k.astype(F32)) * (SB_HEAD_DIM ** -0.5)
         + bias.astype(F32)[None, :, None, None])
    mask = (pos_k[None, :] < pos_q[:, None])[None, None]
    sp = jnp.where(mask, jax.nn.softplus(z), 0.0)
    r = lax.cumsum(sp, axis=3, reverse=True)
    a = jnp.exp(jnp.where(mask, z - r, -jnp.inf))
    o = jnp.einsum('bhqk,bkhd->bqhd', a, v.astype(F32))
    return o.astype(v.dtype)


def sb_prompt(h, w_qkv, w_o, bias):
    b, s, _ = h.shape
    q, k, v = sb_project(h, w_qkv)
    pos = jnp.arange(s, dtype=jnp.int32)
    nb = s // Q_BLOCK
    qb = q.reshape(b, nb, Q_BLOCK, SB_HEADS, SB_HEAD_DIM).transpose(1, 0, 2, 3, 4)
    pb = pos.reshape(nb, Q_BLOCK)
    ob = lax.map(lambda a: sb_attend(a[0], k, v, a[1], pos, bias), (qb, pb))
    o = ob.transpose(1, 0, 2, 3, 4).reshape(b, s, D_MODEL)
    return o @ w_o, k, v


def sb_sample(h, k_pool, v_pool, page_table, w_qkv, w_o, bias):
    b, s, _ = h.shape
    q, k, v = sb_project(h, w_qkv)
    past_len = page_table.shape[1] * PAGE_SIZE
    k_past = k_pool[page_table].reshape(b, past_len, SB_HEADS, SB_HEAD_DIM).astype(k.dtype)
    v_past = v_pool[page_table].reshape(b, past_len, SB_HEADS, SB_HEAD_DIM).astype(v.dtype)
    k_all = jnp.concatenate([k_past, k], axis=1)
    v_all = jnp.concatenate([v_past, v], axis=1)
    pos_k = jnp.arange(past_len + s, dtype=jnp.int32)
    pos_q = past_len + jnp.arange(s, dtype=jnp.int32)
    o = sb_attend(q, k_all, v_all, pos_q, pos_k, bias).reshape(b, s, D_MODEL)
    return o @ w_o, k, v


def rope(x, pos):
    half = x.shape[-1] // 2
    inv = ROPE_BASE ** (-jnp.arange(half, dtype=F32) / half)
    ang = pos.astype(F32)[:, None] * inv[None, :]
    cos = jnp.cos(ang)[:, None, :]
    sin = jnp.sin(ang)[:, None, :]
    x1, x2 = x[..., :half], x[..., half:]
    return jnp.concatenate([x1 * cos - x2 * sin, x1 * sin + x2 * cos], axis=-1)


def ret_log_gamma():
    return jnp.log1p(-jnp.exp2(-5.0 - jnp.arange(RET_HEADS, dtype=F32)))


def ret_project(h, pos, w_qkvg):
    b, s, _ = h.shape
    proj = (h @ w_qkvg).astype(F32)
    q, k, v, g = jnp.split(proj, [D_MODEL, 2 * D_MODEL, 4 * D_MODEL], axis=-1)
    q = rope(q.reshape(b, s, RET_HEADS, RET_QK_DIM), pos) * (RET_QK_DIM ** -0.5)
    k = rope(k.reshape(b, s, RET_HEADS, RET_QK_DIM), pos)
    v = v.reshape(b, s, RET_HEADS, RET_V_DIM)
    return q, k, v, g


def retention_chunk(q, k, v, s0, log_gamma):
    c = q.shape[1]
    idx = jnp.arange(c, dtype=F32)
    diff = idx[:, None] - idx[None, :]
    causal = diff >= 0
    decay = jnp.where(causal[None], jnp.exp(log_gamma[:, None, None] * jnp.where(causal, diff, 0.0)[None]), 0.0)
    scores = jnp.einsum('bnhd,bmhd->bhnm', q, k) * decay[None]
    inner = jnp.einsum('bhnm,bmhe->bnhe', scores, v)
    xi = jnp.exp(log_gamma[None, :] * (idx[:, None] + 1.0))
    cross = jnp.einsum('bnhd,bhde->bnhe', q, s0) * xi[None, :, :, None]
    zeta = jnp.exp(log_gamma[None, :] * (c - 1.0 - idx[:, None]))
    s_new = (jnp.exp(log_gamma * c)[None, :, None, None] * s0
             + jnp.einsum('bmhd,bmhe->bhde', k * zeta[None, :, :, None], v))
    return inner + cross, s_new


def ret_output(o, g, gn_g, w_o, dtype):
    b, s = o.shape[:2]
    mu = jnp.mean(o, axis=-1, keepdims=True)
    var = jnp.mean((o - mu) ** 2, axis=-1, keepdims=True)
    o = ((o - mu) * lax.rsqrt(var + EPS)).reshape(b, s, RET_HEADS * RET_V_DIM) * gn_g.astype(F32)
    return (jax.nn.silu(g) * o).astype(dtype) @ w_o


def ret_prompt(h, w_qkvg, gn_g, w_o, log_gamma):
    b, s, _ = h.shape
    q, k, v, g = ret_project(h, jnp.arange(s, dtype=jnp.int32), w_qkvg)
    nc = s // RET_CHUNK

    def to_chunks(t):
        return t.reshape(b, nc, RET_CHUNK, *t.shape[2:]).swapaxes(0, 1)

    def step(state, chunk):
        o, state = retention_chunk(chunk[0], chunk[1], chunk[2], state, log_gamma)
        return state, o

    s0 = jnp.zeros((b, RET_HEADS, RET_QK_DIM, RET_V_DIM), F32)
    s_final, oc = lax.scan(step, s0, (to_chunks(q), to_chunks(k), to_chunks(v)))
    o = oc.swapaxes(0, 1).reshape(b, s, RET_HEADS, RET_V_DIM)
    return ret_output(o, g, gn_g, w_o, h.dtype), s_final


def ret_sample(h, state, w_qkvg, gn_g, w_o, log_gamma, past_len):
    b, s, _ = h.shape
    pos = past_len + jnp.arange(s, dtype=jnp.int32)
    q, k, v, g = ret_project(h, pos, w_qkvg)
    o, s_new = retention_chunk(q, k, v, state.astype(F32), log_gamma)
    return ret_output(o, g, gn_g, w_o, h.dtype), s_new


def setup_inputs(seed: int = 0) -> dict:
    key = jax.random.key(seed)
    ks = jax.random.split(key, 16)
    n_pages = PAST_LEN // PAGE_SIZE
    n_used = DEC_BATCH * n_pages
    n_pool = n_used + (n_used + 3) // 4
    page_table = jax.random.permutation(ks[0], n_pool)[:n_used].reshape(DEC_BATCH, n_pages).astype(jnp.int32)
    nrm = jax.random.normal
    x_prompt = nrm(ks[1], (BATCH, SEQ, D_MODEL), F32)
    x_sample = nrm(ks[2], (DEC_BATCH, DEC_SEQ, D_MODEL), F32)
    cache_k = nrm(ks[3], (N_SB_LAYERS, n_pool, PAGE_SIZE, SB_HEADS, SB_HEAD_DIM), F32)
    cache_v = nrm(ks[4], (N_SB_LAYERS, n_pool, PAGE_SIZE, SB_HEADS, SB_HEAD_DIM), F32)
    state_ret = nrm(ks[5], (N_RET_LAYERS, DEC_BATCH, RET_HEADS, RET_QK_DIM, RET_V_DIM), F32)
    norm_g = 1.0 + 0.05 * nrm(ks[6], (DEPTH, N_NORMS, D_MODEL), F32)
    ffn_w_gate = nrm(ks[7], (DEPTH, 2, D_MODEL, D_FF), F32) * D_MODEL ** -0.5
    ffn_w_up = nrm(ks[8], (DEPTH, 2, D_MODEL, D_FF), F32) * D_MODEL ** -0.5
    ffn_w_down = nrm(ks[9], (DEPTH, 2, D_FF, D_MODEL), F32) * D_FF ** -0.5
    sb_w_qkv = nrm(ks[10], (N_SB_LAYERS, D_MODEL, 3 * D_MODEL), F32) * D_MODEL ** -0.5
    sb_w_o = nrm(ks[11], (N_SB_LAYERS, D_MODEL, D_MODEL), F32) * D_MODEL ** -0.5
    sb_bias = SB_BIAS_INIT + 0.1 * nrm(ks[15], (N_SB_LAYERS, SB_HEADS), F32)
    ret_w_qkvg = nrm(ks[12], (N_RET_LAYERS, D_MODEL, 6 * D_MODEL), F32) * D_MODEL ** -0.5
    ret_gn_g = 1.0 + 0.05 * nrm(ks[13], (N_RET_LAYERS, RET_HEADS * RET_V_DIM), F32)
    ret_w_o = nrm(ks[14], (N_RET_LAYERS, 2 * D_MODEL, D_MODEL), F32) * (2 * D_MODEL) ** -0.5
    return {"x_prompt": x_prompt, "x_sample": x_sample, "cache_k": cache_k, "cache_v": cache_v,
            "state_ret": state_ret, "page_table": page_table, "norm_g": norm_g,
            "ffn_w_gate": ffn_w_gate, "ffn_w_up": ffn_w_up, "ffn_w_down": ffn_w_down,
            "sb_w_qkv": sb_w_qkv, "sb_w_o": sb_w_o, "sb_bias": sb_bias, "ret_w_qkvg": ret_w_qkvg,
            "ret_gn_g": ret_gn_g, "ret_w_o": ret_w_o}


def reference(x_prompt, x_sample, cache_k, cache_v, state_ret, page_table, norm_g,
              ffn_w_gate, ffn_w_up, ffn_w_down, sb_w_qkv, sb_w_o, sb_bias, ret_w_qkvg, ret_gn_g, ret_w_o):
    log_gamma = ret_log_gamma()
    past_len = page_table.shape[1] * PAGE_SIZE
    xp, xs = x_prompt, x_sample
    kp, vp, ksm, vsm, rsp, rss = [], [], [], [], [], []
    for i in range(DEPTH):
        g = norm_g[i]
        xp = half_ffn(xp, g[0], g[1], ffn_w_gate[i, 0], ffn_w_up[i, 0], ffn_w_down[i, 0])
        xs = half_ffn(xs, g[0], g[1], ffn_w_gate[i, 0], ffn_w_up[i, 0], ffn_w_down[i, 0])
        hp = rmsnorm(xp, g[2])
        hs = rmsnorm(xs, g[2])
        l = i // N_MIXERS
        if i % N_MIXERS == 0:
            mp, k_new, v_new = sb_prompt(hp, sb_w_qkv[l], sb_w_o[l], sb_bias[l])
            ms, k_s, v_s = sb_sample(hs, cache_k[l], cache_v[l], page_table, sb_w_qkv[l], sb_w_o[l], sb_bias[l])
            kp.append(k_new)
            vp.append(v_new)
            ksm.append(k_s)
            vsm.append(v_s)
        else:
            mp, st_p = ret_prompt(hp, ret_w_qkvg[l], ret_gn_g[l], ret_w_o[l], log_gamma)
            ms, st_s = ret_sample(hs, state_ret[l], ret_w_qkvg[l], ret_gn_g[l], ret_w_o[l], log_gamma, past_len)
            rsp.append(st_p.astype(state_ret.dtype))
            rss.append(st_s.astype(state_ret.dtype))
        xp = xp + rmsnorm(mp, g[3])
        xs = xs + rmsnorm(ms, g[3])
        xp = half_ffn(xp, g[4], g[5], ffn_w_gate[i, 1], ffn_w_up[i, 1], ffn_w_down[i, 1])
        xs = half_ffn(xs, g[4], g[5], ffn_w_gate[i, 1], ffn_w_up[i, 1], ffn_w_down[i, 1])
    return (xp, xs, jnp.stack(kp), jnp.stack(vp), jnp.stack(ksm), jnp.stack(vsm), jnp.stack(rsp), jnp.stack(rss))
```

```python
import functools

import jax
import jax.numpy as jnp
from jax import lax
from jax.experimental import pallas as pl
from jax.experimental.pallas import tpu as pltpu

F32 = jnp.float32
BF16 = jnp.bfloat16

EPS = 1e-6
SB_HEADS = 16
SB_HEAD_DIM = 128
PAGE_SIZE = 128
RET_HEADS = 8
RET_QK_DIM = 256
RET_V_DIM = 512
RET_CHUNK = 128
ROPE_BASE = 10000.0

VMEM_LIMIT_BYTES = 56 * 1024 * 1024

ROW_TILE = 1024
DOWN_ROW_TILE = 512
COL_TILE = 512
K_TILE = 512
ATT_BLOCK = 256


def _params(sem):
    return pltpu.CompilerParams(dimension_semantics=sem, vmem_limit_bytes=VMEM_LIMIT_BYTES)


def _rms(x, g):
    return x * lax.rsqrt(jnp.mean(x * x, axis=-1, keepdims=True) + EPS) * g


def _row_tile(m, want):
    return want if m % want == 0 else m


def _prenorm_kernel(x_ref, g_ref, o_ref):
    o_ref[...] = _rms(x_ref[...], g_ref[...]).astype(o_ref.dtype)


def prenorm(x, g):
    m, d = x.shape
    tm = _row_tile(m, DOWN_ROW_TILE)
    return pl.pallas_call(
        _prenorm_kernel,
        grid=(m // tm,),
        in_specs=[pl.BlockSpec((tm, d), lambda i: (i, 0)), pl.BlockSpec((1, d), lambda i: (0, 0))],
        out_specs=pl.BlockSpec((tm, d), lambda i: (i, 0)),
        out_shape=jax.ShapeDtypeStruct((m, d), BF16),
        compiler_params=_params(("arbitrary",)),
        name="prenorm",
    )(x, g.reshape(1, d))


def _swiglu_kernel(a_ref, wg_ref, wu_ref, o_ref, wg_bf, wu_bf):
    @pl.when(pl.program_id(1) == 0)
    def _():
        wg_bf[...] = wg_ref[...].astype(BF16)
        wu_bf[...] = wu_ref[...].astype(BF16)

    a = a_ref[...]
    gate = jnp.dot(a, wg_bf[...], preferred_element_type=F32)
    up = jnp.dot(a, wu_bf[...], preferred_element_type=F32)
    o_ref[...] = (gate * jax.nn.sigmoid(gate) * up).astype(o_ref.dtype)


def swiglu_up(h, w_gate, w_up, widx):
    m, k = h.shape
    n = w_gate.shape[-1]
    tm, tn = _row_tile(m, ROW_TILE), COL_TILE
    i0, i1 = widx
    wspec = pl.BlockSpec((None, None, k, tn), lambda j, i: (i0, i1, 0, j))
    return pl.pallas_call(
        _swiglu_kernel,
        grid=(n // tn, m // tm),
        in_specs=[pl.BlockSpec((tm, k), lambda j, i: (i, 0)), wspec, wspec],
        out_specs=pl.BlockSpec((tm, tn), lambda j, i: (i, j)),
        out_shape=jax.ShapeDtypeStruct((m, n), BF16),
        scratch_shapes=[pltpu.VMEM((k, tn), BF16), pltpu.VMEM((k, tn), BF16)],
        compiler_params=_params(("arbitrary", "arbitrary")),
        name="swiglu_up",
    )(h, w_gate, w_up)


def _proj_kernel(*refs, rope, out_scale):
    if rope:
        a_ref, w_ref, cos_ref, sin_ref, o_ref, w_bf = refs
    else:
        a_ref, w_ref, o_ref, w_bf = refs

    @pl.when(pl.program_id(1) == 0)
    def _():
        w_bf[...] = w_ref[...].astype(BF16)

    y = jnp.dot(a_ref[...], w_bf[...], preferred_element_type=F32)
    if rope:
        cos, sin = cos_ref[...], sin_ref[...]
        half = cos.shape[-1]
        for hd in range(y.shape[-1] // (2 * half)):
            lo = hd * 2 * half
            x1 = y[:, lo:lo + half]
            x2 = y[:, lo + half:lo + 2 * half]
            o_ref[:, lo:lo + half] = ((x1 * cos - x2 * sin) * out_scale).astype(o_ref.dtype)
            o_ref[:, lo + half:lo + 2 * half] = ((x1 * sin + x2 * cos) * out_scale).astype(o_ref.dtype)
    else:
        o_ref[...] = y.astype(o_ref.dtype)


def project(h, w, col0, ncols, out_dtype, rope=None, out_scale=1.0):
    m, k = h.shape
    tm, tn = _row_tile(m, ROW_TILE), COL_TILE
    off = col0 // tn
    in_specs = [pl.BlockSpec((tm, k), lambda j, i: (i, 0)), pl.BlockSpec((k, tn), lambda j, i: (0, off + j))]
    args = [h, w]
    if rope is not None:
        half = rope[0].shape[-1]
        tspec = pl.BlockSpec((tm, half), lambda j, i: (i, 0))
        in_specs += [tspec, tspec]
        args += list(rope)
    return pl.pallas_call(
        functools.partial(_proj_kernel, rope=rope is not None, out_scale=out_scale),
        grid=(ncols // tn, m // tm),
        in_specs=in_specs,
        out_specs=pl.BlockSpec((tm, tn), lambda j, i: (i, j)),
        out_shape=jax.ShapeDtypeStruct((m, ncols), out_dtype),
        scratch_shapes=[pltpu.VMEM((k, tn), BF16)],
        compiler_params=_params(("arbitrary", "arbitrary")),
        name="project_rope" if rope is not None else "project",
    )(*args)


def _down_kernel(*refs, res_scale, nk, has_next):
    if has_next:
        a_ref, w_ref, x_ref, gpost_ref, gnext_ref, xo_ref, ho_ref, acc = refs
    else:
        a_ref, w_ref, x_ref, gpost_ref, xo_ref, acc = refs
    kk = pl.program_id(1)

    @pl.when(kk == 0)
    def _():
        acc[...] = jnp.zeros_like(acc)

    acc[...] += jnp.dot(a_ref[...].astype(BF16), w_ref[...], preferred_element_type=F32)

    @pl.when(kk == nk - 1)
    def _():
        xn = x_ref[...] + res_scale * _rms(acc[...], gpost_ref[...])
        xo_ref[...] = xn
        if has_next:
            ho_ref[...] = _rms(xn, gnext_ref[...]).astype(ho_ref.dtype)


def down_residual(a, w, widx, x, g_post, g_next, res_scale):
    m, k = a.shape
    d = w.shape[-1]
    tm, tk = _row_tile(m, DOWN_ROW_TILE), K_TILE
    nk = k // tk
    i0, i1 = widx
    has_next = g_next is not None
    row = pl.BlockSpec((tm, d), lambda i, kk: (i, 0))
    vec = pl.BlockSpec((1, d), lambda i, kk: (0, 0))
    in_specs = [pl.BlockSpec((tm, tk), lambda i, kk: (i, kk)),
                pl.BlockSpec((None, None, tk, d), lambda i, kk: (i0, i1, kk, 0)), row, vec]
    args = [a, w, x, g_post.reshape(1, d)]
    out_specs = [row]
    out_shape = [jax.ShapeDtypeStruct((m, d), F32)]
    if has_next:
        in_specs.append(vec)
        args.append(g_next.reshape(1, d))
        out_specs.append(row)
        out_shape.append(jax.ShapeDtypeStruct((m, d), BF16))
    outs = pl.pallas_call(
        functools.partial(_down_kernel, res_scale=res_scale, nk=nk, has_next=has_next),
        grid=(m // tm, nk),
        in_specs=in_specs,
        out_specs=out_specs,
        out_shape=out_shape,
        scratch_shapes=[pltpu.VMEM((tm, d), F32)],
        compiler_params=_params(("arbitrary", "arbitrary")),
        name="down_residual",
    )(*args)
    return (outs[0], outs[1]) if has_next else (outs[0], None)


def _softplus(z):
    return jnp.maximum(z, 0.0) + jnp.log1p(jnp.exp(-jnp.abs(z)))


def _suffix_sum(sp, tri):
    hi = sp.astype(BF16)
    lo = (sp - hi.astype(F32)).astype(BF16)
    return jnp.dot(hi, tri, preferred_element_type=F32) + jnp.dot(lo, tri, preferred_element_type=F32)


def _sb_prompt_kernel(bias_ref, q_ref, k_ref, v_ref, o_ref, k_bf, v_bf, acc, run, *, blk, scale):
    h = pl.program_id(1)
    qi = pl.program_id(2)

    @pl.when(qi == 0)
    def _():
        k_bf[...] = k_ref[...].astype(BF16)
        v_bf[...] = v_ref[...].astype(BF16)

    q = q_ref[...]
    bias = bias_ref[h]
    row = lax.broadcasted_iota(jnp.int32, (blk, blk), 0)
    col = lax.broadcasted_iota(jnp.int32, (blk, blk), 1)
    tri = (row >= col).astype(BF16)
    visible = col < row

    acc[...] = jnp.zeros_like(acc)
    run[...] = jnp.zeros_like(run)

    def tile(start, diagonal):
        kt = k_bf[pl.ds(start, blk), :]
        vt = v_bf[pl.ds(start, blk), :]
        z = lax.dot_general(q, kt, (((1,), (1,)), ((), ())), preferred_element_type=F32) * scale + bias
        sp = _softplus(z)
        if diagonal:
            sp = jnp.where(visible, sp, 0.0)
        local = _suffix_sum(sp, tri)
        a = jnp.exp(z - (local + run[...]))
        if diagonal:
            a = jnp.where(visible, a, 0.0)
        acc[...] += jnp.dot(a.astype(BF16), vt, preferred_element_type=F32)
        run[...] += local[:, 0:1]

    tile(pl.multiple_of(qi * blk, blk), True)

    def body(j, carry):
        tile(pl.multiple_of((qi - 1 - j) * blk, blk), False)
        return carry

    lax.fori_loop(0, qi, body, 0)
    o_ref[...] = acc[...].astype(o_ref.dtype)


def sb_attention_prompt(q, k, v, bias, batch, seq):
    blk = ATT_BLOCK
    nq = seq // blk
    dh = SB_HEAD_DIM
    qspec = pl.BlockSpec((blk, dh), lambda b, h, i: (b * nq + i, h))
    kvspec = pl.BlockSpec((seq, dh), lambda b, h, i: (b, h))
    return pl.pallas_call(
        functools.partial(_sb_prompt_kernel, blk=blk, scale=dh ** -0.5),
        grid=(batch, SB_HEADS, nq),
        in_specs=[pl.BlockSpec(memory_space=pltpu.SMEM), qspec, kvspec, kvspec],
        out_specs=qspec,
        out_shape=jax.ShapeDtypeStruct(q.shape, BF16),
        scratch_shapes=[pltpu.VMEM((seq, dh), BF16), pltpu.VMEM((seq, dh), BF16),
                        pltpu.VMEM((blk, dh), F32), pltpu.VMEM((blk, 1), F32)],
        compiler_params=_params(("arbitrary", "arbitrary", "arbitrary")),
        name="sb_attention_prompt",
    )(bias, q, k, v)


LANES = 128


def _sb_decode_kernel(pt_ref, q_ref, bias_ref, k_ref, v_ref, o_ref, acc, run, z_rows, a_flat, g_rows,
                      *, scale, n_pages):
    del pt_ref
    p = pl.program_id(1)
    page, heads, dh = k_ref.shape
    flat = page * heads
    n_groups = flat // LANES

    @pl.when(p == 0)
    def _():
        acc[...] = jnp.zeros_like(acc)
        run[...] = jnp.zeros_like(run)

    lane = lax.broadcasted_iota(jnp.int32, (heads, flat), 1)
    own = lax.rem(lane, heads) == lax.broadcasted_iota(jnp.int32, (heads, flat), 0)

    kf = k_ref[...].reshape(flat, dh).astype(BF16)
    zt = lax.dot_general(q_ref[...].astype(BF16), kf, (((1,), (1,)), ((), ())), preferred_element_type=F32)
    zf = jnp.sum(jnp.where(own, zt, 0.0), axis=0, keepdims=True)
    for c in range(n_groups):
        z_rows[c:c + 1, :] = zf[:, c * LANES:(c + 1) * LANES]
    z = z_rows[...] * scale + bias_ref[...]
    sp = _softplus(z)

    src = lax.broadcasted_iota(jnp.int32, (LANES, 2 * LANES), 0)
    dst = lax.broadcasted_iota(jnp.int32, (LANES, 2 * LANES), 1)
    same_head = lax.rem(src, heads) == lax.rem(dst, heads)
    newer = (src // heads) >= (lax.rem(dst, LANES) // heads)
    sel = (same_head & (newer | (dst >= LANES))).astype(BF16)
    both = _suffix_sum(sp, sel)
    local, group_total = both[:, :LANES], both[:, LANES:]
    newer_groups = jnp.zeros((1, LANES), F32)
    for c in range(n_groups - 1, -1, -1):
        g_rows[c:c + 1, :] = newer_groups
        newer_groups = newer_groups + group_total[c:c + 1, :]

    a = jnp.exp(z - (local + g_rows[...] + run[...]))
    run[...] += newer_groups
    for c in range(n_groups):
        a_flat[:, c * LANES:(c + 1) * LANES] = a[c:c + 1, :]
    a_own = jnp.where(own, jnp.broadcast_to(a_flat[...], (heads, flat)), 0.0).astype(BF16)
    vf = v_ref[...].reshape(flat, dh).astype(BF16)
    acc[...] += jnp.dot(a_own, vf, preferred_element_type=F32)

    @pl.when(p == n_pages - 1)
    def _():
        o_ref[...] = acc[...]


def sb_attention_decode(q, k_pool, v_pool, layer, page_table, bias):
    b, heads, dh = q.shape
    n_pages = page_table.shape[1]
    page = k_pool.shape[2]
    n_groups = page * heads // LANES

    def page_map(bi, p, pt):
        return (layer, pt[bi * n_pages + (n_pages - 1 - p)], 0, 0, 0)

    pool_spec = pl.BlockSpec((None, None, page, heads, dh), page_map)
    q_spec = pl.BlockSpec((None, heads, dh), lambda bi, p, pt: (bi, 0, 0))
    bias_lanes = jnp.tile(bias, LANES // heads).reshape(1, LANES)
    return pl.pallas_call(
        functools.partial(_sb_decode_kernel, scale=dh ** -0.5, n_pages=n_pages),
        grid_spec=pltpu.PrefetchScalarGridSpec(
            num_scalar_prefetch=1,
            grid=(b, n_pages),
            in_specs=[q_spec, pl.BlockSpec((1, LANES), lambda bi, p, pt: (0, 0)), pool_spec, pool_spec],
            out_specs=q_spec,
            scratch_shapes=[pltpu.VMEM((heads, dh), F32), pltpu.VMEM((1, LANES), F32),
                            pltpu.VMEM((n_groups, LANES), F32), pltpu.VMEM((1, page * heads), F32),
                            pltpu.VMEM((n_groups, LANES), F32)],
        ),
        out_shape=jax.ShapeDtypeStruct((b, heads, dh), F32),
        compiler_params=_params(("arbitrary", "arbitrary")),
        name="sb_attention_decode",
    )(page_table.reshape(-1), q, bias_lanes, k_pool, v_pool)


def _group_norm_gate(o, gate, gn):
    mu = jnp.mean(o, axis=-1, keepdims=True)
    var = jnp.mean((o - mu) ** 2, axis=-1, keepdims=True)
    return (o - mu) * lax.rsqrt(var + EPS) * gn * (gate * jax.nn.sigmoid(gate))


def _ret_prompt_kernel(lg_ref, q_ref, k_ref, v_ref, g_ref, gn_ref, o_ref, s_ref, state, *, chunk, n_chunks):
    lg = lg_ref[pl.program_id(1)]
    n = lax.broadcasted_iota(jnp.int32, (chunk, chunk), 0)
    mcol = lax.broadcasted_iota(jnp.int32, (chunk, chunk), 1)
    diff = (n - mcol).astype(F32)
    decay = jnp.where(diff >= 0, jnp.exp(lg * jnp.maximum(diff, 0.0)), 0.0)
    idx = lax.broadcasted_iota(jnp.int32, (chunk, 1), 0).astype(F32)
    xi = jnp.exp(lg * (idx + 1.0))
    zeta = jnp.exp(lg * (chunk - 1.0 - idx))
    chunk_decay = jnp.exp(lg * jnp.full((1, state.shape[1]), float(chunk), F32))
    gn = gn_ref[...]

    state[...] = jnp.zeros_like(state)

    def body(c, carry):
        rows = pl.ds(pl.multiple_of(c * chunk, chunk), chunk)
        qc, kc, vc = q_ref[rows, :], k_ref[rows, :], v_ref[rows, :]
        s0 = state[...]
        scores = lax.dot_general(qc, kc, (((1,), (1,)), ((), ())), preferred_element_type=F32) * decay
        inner = jnp.dot(scores.astype(BF16), vc, preferred_element_type=F32)
        cross = jnp.dot(qc, s0.astype(BF16), preferred_element_type=F32) * xi
        kz = (kc.astype(F32) * zeta).astype(BF16)
        state[...] = chunk_decay * s0 + lax.dot_general(kz, vc, (((0,), (0,)), ((), ())),
                                                        preferred_element_type=F32)
        o_ref[rows, :] = _group_norm_gate(inner + cross, g_ref[rows, :], gn).astype(o_ref.dtype)
        return carry

    lax.fori_loop(0, n_chunks, body, 0)
    s_ref[...] = state[...]


def retention_prompt(q, k, v, g, gn, log_gamma, batch, seq):
    dk, dv = RET_QK_DIM, RET_V_DIM
    qk_spec = pl.BlockSpec((seq, dk), lambda b, h: (b, h))
    v_spec = pl.BlockSpec((seq, dv), lambda b, h: (b, h))
    return pl.pallas_call(
        functools.partial(_ret_prompt_kernel, chunk=RET_CHUNK, n_chunks=seq // RET_CHUNK),
        grid=(batch, RET_HEADS),
        in_specs=[pl.BlockSpec(memory_space=pltpu.SMEM), qk_spec, qk_spec, v_spec, v_spec,
                  pl.BlockSpec((1, dv), lambda b, h: (0, h))],
        out_specs=[v_spec, pl.BlockSpec((None, None, dk, dv), lambda b, h: (b, h, 0, 0))],
        out_shape=[jax.ShapeDtypeStruct(v.shape, BF16),
                   jax.ShapeDtypeStruct((batch, RET_HEADS, dk, dv), F32)],
        scratch_shapes=[pltpu.VMEM((dk, dv), F32)],
        compiler_params=_params(("arbitrary", "arbitrary")),
        name="retention_prompt",
    )(log_gamma, q, k, v, g, gn.reshape(1, -1))


def _ret_decode_kernel(lg_ref, q_ref, k_ref, v_ref, g_ref, gn_ref, s0_ref, o_ref, s_ref):
    lg = lg_ref[pl.program_id(1)]
    dv = v_ref.shape[-1]
    gamma = jnp.exp(lg * jnp.ones((1, dv), F32))
    q, k, v, s0 = q_ref[...], k_ref[...], v_ref[...], s0_ref[...]
    qk = jnp.sum(q * k, axis=0, keepdims=True)
    cross = jnp.sum(q * s0, axis=0, keepdims=True) * gamma
    o = qk * v + cross
    s_ref[...] = gamma * s0 + k * v
    o_ref[...] = _group_norm_gate(o, g_ref[...], gn_ref[...]).astype(o_ref.dtype)


def retention_decode(q, k, v, g, gn, log_gamma, state):
    b = q.shape[0]
    h, dk, dv = RET_HEADS, RET_QK_DIM, RET_V_DIM
    col_spec = pl.BlockSpec((None, None, dk, 1), lambda bi, hi: (bi, hi, 0, 0))
    row_spec = pl.BlockSpec((None, None, 1, dv), lambda bi, hi: (bi, hi, 0, 0))
    st_spec = pl.BlockSpec((None, None, dk, dv), lambda bi, hi: (bi, hi, 0, 0))
    o, s_new = pl.pallas_call(
        _ret_decode_kernel,
        grid=(b, h),
        in_specs=[pl.BlockSpec(memory_space=pltpu.SMEM), col_spec, col_spec, row_spec, row_spec,
                  pl.BlockSpec((None, 1, dv), lambda bi, hi: (hi, 0, 0)), st_spec],
        out_specs=[row_spec, st_spec],
        out_shape=[jax.ShapeDtypeStruct((b, h, 1, dv), BF16), jax.ShapeDtypeStruct(state.shape, F32)],
        compiler_params=_params(("arbitrary", "arbitrary")),
        name="retention_decode",
    )(log_gamma, q.reshape(b, h, dk, 1), k.reshape(b, h, dk, 1), v.reshape(b, h, 1, dv),
      g.reshape(b, h, 1, dv), gn.reshape(h, 1, dv), state)
    return o.reshape(b, h * dv), s_new


def _stack(xs):
    return xs[0][None] if len(xs) == 1 else jnp.stack(xs)


def _rope_tables(pos, rows):
    half = RET_QK_DIM // 2
    inv = ROPE_BASE ** (-jnp.arange(half, dtype=F32) / half)
    ang = pos.astype(F32)[:, None] * inv[None, :]
    cos, sin = jnp.cos(ang), jnp.sin(ang)
    reps = rows // pos.shape[0]
    return jnp.tile(cos, (reps, 1)), jnp.tile(sin, (reps, 1))


def kernel(x_prompt, x_sample, cache_k, cache_v, state_ret, page_table, norm_g, ffn_w_gate, ffn_w_up, ffn_w_down,
           sb_w_qkv, sb_w_o, sb_bias, ret_w_qkvg, ret_gn_g, ret_w_o):
    batch, seq, d = x_prompt.shape
    dec_batch, dec_seq, _ = x_sample.shape
    depth = norm_g.shape[0]
    past_len = page_table.shape[1] * PAGE_SIZE
    assert dec_seq == 1, "the decode kernels take one new token per sequence"
    mp, ms = batch * seq, dec_batch * dec_seq

    log_gamma = jnp.log1p(-jnp.exp2(-5.0 - jnp.arange(RET_HEADS, dtype=F32)))
    rope_p = _rope_tables(jnp.arange(seq, dtype=jnp.int32), mp)
    rope_s = _rope_tables(past_len + jnp.arange(dec_seq, dtype=jnp.int32), ms)

    w_down = ffn_w_down.astype(BF16)
    w_sb_o = sb_w_o.astype(BF16)[None]
    w_ret_o = ret_w_o.astype(BF16)[None]

    xp, xs = x_prompt.reshape(mp, d), x_sample.reshape(ms, d)
    hp, hs = prenorm(xp, norm_g[0, 0]), prenorm(xs, norm_g[0, 0])
    kp, vp, ksm, vsm, rsp, rss = [], [], [], [], [], []

    for i in range(depth):
        g = norm_g[i]
        layer = i // 2

        def ffn(x, h, j, g_post, g_next):
            act = swiglu_up(h, ffn_w_gate, ffn_w_up, (i, j))
            return down_residual(act, w_down, (i, j), x, g_post, g_next, 0.5)

        xp, hp = ffn(xp, hp, 0, g[1], g[2])
        xs, hs = ffn(xs, hs, 0, g[1], g[2])

        if i % 2 == 0:
            w = sb_w_qkv[layer]
            bias = sb_bias[layer]
            q = project(hp, w, 0, d, BF16)
            k = project(hp, w, d, d, F32)
            v = project(hp, w, 2 * d, d, F32)
            mix_p = sb_attention_prompt(q, k, v, bias, batch, seq)
            kp.append(k.reshape(batch, seq, SB_HEADS, SB_HEAD_DIM))
            vp.append(v.reshape(batch, seq, SB_HEADS, SB_HEAD_DIM))

            q_s = project(hs, w, 0, d, F32)
            k_s = project(hs, w, d, d, F32)
            v_s = project(hs, w, 2 * d, d, F32)
            mix_s = sb_attention_decode(q_s.reshape(ms, SB_HEADS, SB_HEAD_DIM), cache_k, cache_v, layer,
                                        page_table, bias).reshape(ms, d)
            ksm.append(k_s.reshape(dec_batch, dec_seq, SB_HEADS, SB_HEAD_DIM))
            vsm.append(v_s.reshape(dec_batch, dec_seq, SB_HEADS, SB_HEAD_DIM))
            w_o = (w_sb_o, (0, layer))
        else:
            w = ret_w_qkvg[layer]
            gn = ret_gn_g[layer]
            q = project(hp, w, 0, d, BF16, rope=rope_p, out_scale=RET_QK_DIM ** -0.5)
            k = project(hp, w, d, d, BF16, rope=rope_p)
            v = project(hp, w, 2 * d, 2 * d, BF16)
            gate = project(hp, w, 4 * d, 2 * d, F32)
            mix_p, st_p = retention_prompt(q, k, v, gate, gn, log_gamma, batch, seq)
            rsp.append(st_p)

            q_s = project(hs, w, 0, d, F32, rope=rope_s, out_scale=RET_QK_DIM ** -0.5)
            k_s = project(hs, w, d, d, F32, rope=rope_s)
            v_s = project(hs, w, 2 * d, 2 * d, F32)
            gate_s = project(hs, w, 4 * d, 2 * d, F32)
            mix_s, st_s = retention_decode(q_s, k_s, v_s, gate_s, gn, log_gamma, state_ret[layer])
            rss.append(st_s)
            w_o = (w_ret_o, (0, layer))

        xp, hp = down_residual(mix_p, w_o[0], w_o[1], xp, g[3], g[4], 1.0)
        xs, hs = down_residual(mix_s, w_o[0], w_o[1], xs, g[3], g[4], 1.0)

        g_next = norm_g[i + 1, 0] if i + 1 < depth else None
        xp, hp = ffn(xp, hp, 1, g[5], g_next)
        xs, hs = ffn(xs, hs, 1, g[5], g_next)

    return (xp.reshape(batch, seq, d), xs.reshape(dec_batch, dec_seq, d), _stack(kp), _stack(vp),
            _stack(ksm), _stack(vsm), _stack(rsp), _stack(rss))
```

```python
import functools

import jax
import jax.numpy as jnp
from jax import lax
from jax.experimental import pallas as pl
from jax.experimental.pallas import tpu as pltpu

F32 = jnp.float32
BF16 = jnp.bfloat16

EPS = 1e-6
SB_HEADS = 16
SB_HEAD_DIM = 128
PAGE_SIZE = 128
RET_HEADS = 8
RET_QK_DIM = 256
RET_V_DIM = 512
RET_CHUNK = 128
ROPE_BASE = 10000.0

VMEM_LIMIT_BYTES = 60 * 1024 * 1024
LANES = 128

ROW_TILE = 1024
DOWN_ROW_TILE = 1024
DOWN_ROW_SPLITS = 2
COL_TILE = 512
QKV_COL_TILE = 256
K_TILE = 512
ATT_BLOCK = 256
DECODE_PAGES_PER_STEP = 4


def _params(sem):
    return pltpu.CompilerParams(dimension_semantics=sem, vmem_limit_bytes=VMEM_LIMIT_BYTES)


def _rms(x, g):
    return x * lax.rsqrt(jnp.mean(x * x, axis=-1, keepdims=True) + EPS) * g


def _row_tile(m, want):
    return want if m % want == 0 else m


def _prenorm_kernel(x_ref, g_ref, o_ref):
    o_ref[...] = _rms(x_ref[...], g_ref[...]).astype(o_ref.dtype)


def prenorm(x, g):
    m, d = x.shape
    tm = _row_tile(m, DOWN_ROW_TILE)
    return pl.pallas_call(
        _prenorm_kernel,
        grid=(m // tm,),
        in_specs=[pl.BlockSpec((tm, d), lambda i: (i, 0)), pl.BlockSpec((1, d), lambda i: (0, 0))],
        out_specs=pl.BlockSpec((tm, d), lambda i: (i, 0)),
        out_shape=jax.ShapeDtypeStruct((m, d), BF16),
        compiler_params=_params(("arbitrary",)),
        name="prenorm",
    )(x, g.reshape(1, d))


def _swiglu_kernel(a_ref, wg_ref, wu_ref, o_ref, wg_bf, wu_bf):
    @pl.when(pl.program_id(1) == 0)
    def _():
        wg_bf[...] = wg_ref[...].astype(BF16)
        wu_bf[...] = wu_ref[...].astype(BF16)

    a = a_ref[...]
    gate = jnp.dot(a, wg_bf[...], preferred_element_type=F32)
    up = jnp.dot(a, wu_bf[...], preferred_element_type=F32)
    o_ref[...] = (gate * jax.nn.sigmoid(gate) * up).astype(o_ref.dtype)


def swiglu_up(h, w_gate, w_up, widx):
    m, k = h.shape
    n = w_gate.shape[-1]
    tm, tn = _row_tile(m, ROW_TILE), COL_TILE
    i0, i1 = widx
    wspec = pl.BlockSpec((None, None, k, tn), lambda j, i: (i0, i1, 0, j))
    return pl.pallas_call(
        _swiglu_kernel,
        grid=(n // tn, m // tm),
        in_specs=[pl.BlockSpec((tm, k), lambda j, i: (i, 0)), wspec, wspec],
        out_specs=pl.BlockSpec((tm, tn), lambda j, i: (i, j)),
        out_shape=jax.ShapeDtypeStruct((m, n), BF16),
        scratch_shapes=[pltpu.VMEM((k, tn), BF16), pltpu.VMEM((k, tn), BF16)],
        compiler_params=_params(("arbitrary", "arbitrary")),
        name="swiglu_up",
    )(h, w_gate, w_up)


def _proj_kernel(*refs, n_out, ropes):
    a_ref = refs[0]
    w_refs = refs[1:1 + n_out]
    has_rope = any(r is not None for r in ropes)
    pos = 1 + n_out
    if has_rope:
        cos_ref, sin_ref = refs[pos:pos + 2]
        pos += 2
    o_refs = refs[pos:pos + n_out]
    w_bfs = refs[pos + n_out:]

    @pl.when(pl.program_id(1) == 0)
    def _():
        for w_ref, w_bf in zip(w_refs, w_bfs):
            w_bf[...] = w_ref[...].astype(BF16)

    a = a_ref[...]
    for o_ref, w_bf, rope_scale in zip(o_refs, w_bfs, ropes):
        y = jnp.dot(a, w_bf[...], preferred_element_type=F32)
        if rope_scale is None:
            o_ref[...] = y.astype(o_ref.dtype)
            continue
        cos, sin = cos_ref[...], sin_ref[...]
        half = cos.shape[-1]
        for hd in range(y.shape[-1] // (2 * half)):
            lo = hd * 2 * half
            x1 = y[:, lo:lo + half]
            x2 = y[:, lo + half:lo + 2 * half]
            o_ref[:, lo:lo + half] = ((x1 * cos - x2 * sin) * rope_scale).astype(o_ref.dtype)
            o_ref[:, lo + half:lo + 2 * half] = ((x1 * sin + x2 * cos) * rope_scale).astype(o_ref.dtype)


def project(h, w, outs, ncols, tn, rope=None):
    m, k = h.shape
    tm = _row_tile(m, ROW_TILE)
    ropes = tuple(o[2] for o in outs)
    in_specs = [pl.BlockSpec((tm, k), lambda j, i: (i, 0))]
    for c0, _, _ in outs:
        in_specs.append(pl.BlockSpec((k, tn), functools.partial(lambda j, i, off: (0, off + j), off=c0 // tn)))
    args = [h] + [w] * len(outs)
    if rope is not None:
        tspec = pl.BlockSpec((tm, rope[0].shape[-1]), lambda j, i: (i, 0))
        in_specs += [tspec, tspec]
        args += list(rope)
    return pl.pallas_call(
        functools.partial(_proj_kernel, n_out=len(outs), ropes=ropes),
        grid=(ncols // tn, m // tm),
        in_specs=in_specs,
        out_specs=[pl.BlockSpec((tm, tn), lambda j, i: (i, j))] * len(outs),
        out_shape=[jax.ShapeDtypeStruct((m, ncols), o[1]) for o in outs],
        scratch_shapes=[pltpu.VMEM((k, tn), BF16)] * len(outs),
        compiler_params=_params(("arbitrary", "arbitrary")),
        name="project_rope" if rope is not None else "project",
    )(*args)


def _down_kernel(*refs, res_scale, nk, has_next, row_splits):
    if has_next:
        a_ref, w_ref, x_ref, gpost_ref, gnext_ref, xo_ref, ho_ref = refs
    else:
        a_ref, w_ref, x_ref, gpost_ref, xo_ref = refs
    kk = pl.program_id(1)
    rows = xo_ref.shape[0] // row_splits

    @pl.when(kk == 0)
    def _():
        xo_ref[...] = jnp.zeros_like(xo_ref)

    w = w_ref[...]
    for r in range(row_splits):
        sl = slice(r * rows, (r + 1) * rows)
        xo_ref[sl, :] += jnp.dot(a_ref[sl, :].astype(BF16), w, preferred_element_type=F32)

    @pl.when(kk == nk - 1)
    def _():
        for r in range(row_splits):
            sl = slice(r * rows, (r + 1) * rows)
            xn = x_ref[sl, :] + res_scale * _rms(xo_ref[sl, :], gpost_ref[...])
            xo_ref[sl, :] = xn
            if has_next:
                ho_ref[sl, :] = _rms(xn, gnext_ref[...]).astype(ho_ref.dtype)


def down_residual(a, w, widx, x, g_post, g_next, res_scale):
    m, k = a.shape
    d = w.shape[-1]
    tm, tk = _row_tile(m, DOWN_ROW_TILE), K_TILE
    row_splits = DOWN_ROW_SPLITS if tm % (8 * DOWN_ROW_SPLITS) == 0 and tm > 8 * DOWN_ROW_SPLITS else 1
    nk = k // tk
    i0, i1 = widx
    has_next = g_next is not None
    row = pl.BlockSpec((tm, d), lambda i, kk: (i, 0))
    vec = pl.BlockSpec((1, d), lambda i, kk: (0, 0))
    in_specs = [pl.BlockSpec((tm, tk), lambda i, kk: (i, kk)),
                pl.BlockSpec((None, None, tk, d), lambda i, kk: (i0, i1, kk, 0)), row, vec]
    args = [a, w, x, g_post.reshape(1, d)]
    out_specs = [row]
    out_shape = [jax.ShapeDtypeStruct((m, d), F32)]
    if has_next:
        in_specs.append(vec)
        args.append(g_next.reshape(1, d))
        out_specs.append(row)
        out_shape.append(jax.ShapeDtypeStruct((m, d), BF16))
    outs = pl.pallas_call(
        functools.partial(_down_kernel, res_scale=res_scale, nk=nk, has_next=has_next, row_splits=row_splits),
        grid=(m // tm, nk),
        in_specs=in_specs,
        out_specs=out_specs,
        out_shape=out_shape,
        compiler_params=_params(("arbitrary", "arbitrary")),
        name="down_residual",
    )(*args)
    return (outs[0], outs[1]) if has_next else (outs[0], None)


def _softplus(z):
    return jnp.maximum(z, 0.0) + jnp.log(1.0 + jnp.exp(-jnp.abs(z)))


def _suffix_sum(sp, tri):
    hi = sp.astype(BF16)
    lo = (sp - hi.astype(F32)).astype(BF16)
    return jnp.dot(hi, tri, preferred_element_type=F32) + jnp.dot(lo, tri, preferred_element_type=F32)


def _sb_prompt_kernel(bias_ref, q_ref, k_ref, v_ref, o_ref, acc, run, *, blk, scale):
    nblk = q_ref.shape[0] // blk
    bias = bias_ref[pl.program_id(1)]
    row = lax.broadcasted_iota(jnp.int32, (blk, blk), 0)
    col = lax.broadcasted_iota(jnp.int32, (blk, blk), 1)
    tri = (row >= col).astype(BF16)
    visible = col < row
    k_bf = k_ref[...].astype(BF16)
    v_bf = v_ref[...].astype(BF16)

    def mask_diagonal(x):
        top = jnp.where(visible, x[:blk], 0.0)
        return top if x.shape[0] == blk else jnp.concatenate([top, x[blk:]], axis=0)

    acc[...] = jnp.zeros_like(acc)
    run[...] = jnp.zeros_like(run)

    for j in reversed(range(nblk)):
        r0 = j * blk
        z = lax.dot_general(q_ref[r0:, :], k_bf[r0:r0 + blk], (((1,), (1,)), ((), ())),
                            preferred_element_type=F32) * scale + bias
        local = _suffix_sum(mask_diagonal(_softplus(z)), tri)
        p = mask_diagonal(jnp.exp(z - local))
        pv = jnp.dot(p.astype(BF16), v_bf[r0:r0 + blk], preferred_element_type=F32)
        acc[r0:, :] += jnp.exp(-run[r0:, :]) * pv
        run[r0:, :] += local[:, 0:1]

    o_ref[...] = acc[...].astype(o_ref.dtype)


def sb_attention_prompt(q, k, v, bias, batch, seq):
    dh = SB_HEAD_DIM
    spec = pl.BlockSpec((seq, dh), lambda b, h: (b, h))
    return pl.pallas_call(
        functools.partial(_sb_prompt_kernel, blk=ATT_BLOCK, scale=dh ** -0.5),
        grid=(batch, SB_HEADS),
        in_specs=[pl.BlockSpec(memory_space=pltpu.SMEM), spec, spec, spec],
        out_specs=spec,
        out_shape=jax.ShapeDtypeStruct(q.shape, BF16),
        scratch_shapes=[pltpu.VMEM((seq, dh), F32), pltpu.VMEM((seq, 1), F32)],
        compiler_params=_params(("arbitrary", "arbitrary")),
        name="sb_attention_prompt",
    )(bias, q, k, v)


def _sb_decode_kernel(pt_ref, q_ref, bias_ref, *refs, scale, n_steps, group):
    del pt_ref
    k_refs, v_refs = refs[:group], refs[group:2 * group]
    o_ref, acc, run, z_rows, a_flat, g_rows = refs[2 * group:]
    p = pl.program_id(1)
    page, heads, dh = k_refs[0].shape
    flat = page * heads
    n_groups = flat // LANES

    @pl.when(p == 0)
    def _():
        acc[...] = jnp.zeros_like(acc)
        run[...] = jnp.zeros_like(run)

    lane = lax.broadcasted_iota(jnp.int32, (heads, flat), 1)
    own = lax.rem(lane, heads) == lax.broadcasted_iota(jnp.int32, (heads, flat), 0)
    src = lax.broadcasted_iota(jnp.int32, (LANES, 2 * LANES), 0)
    dst = lax.broadcasted_iota(jnp.int32, (LANES, 2 * LANES), 1)
    same_head = lax.rem(src, heads) == lax.rem(dst, heads)
    newer = (src // heads) >= (lax.rem(dst, LANES) // heads)
    sel = (same_head & (newer | (dst >= LANES))).astype(BF16)
    q = q_ref[...].astype(BF16)
    bias = bias_ref[...]

    def logits(g):
        kf = k_refs[g][...].reshape(flat, dh).astype(BF16)
        zt = lax.dot_general(q, kf, (((1,), (1,)), ((), ())), preferred_element_type=F32)
        zf = jnp.sum(jnp.where(own, zt, 0.0), axis=0, keepdims=True)
        for c in range(n_groups):
            z_rows[g, c:c + 1, :] = zf[:, c * LANES:(c + 1) * LANES]
        z = z_rows[g] * scale + bias
        both = _suffix_sum(_softplus(z), sel)
        local, group_total = both[:, :LANES], both[:, LANES:]
        newer_groups = jnp.zeros((1, LANES), F32)
        for c in range(n_groups - 1, -1, -1):
            g_rows[g, c:c + 1, :] = newer_groups
            newer_groups = newer_groups + group_total[c:c + 1, :]
        return z - (local + g_rows[g]), newer_groups

    pages = [logits(g) for g in range(group)]
    seen = run[...]
    for g, (log_a, page_total) in enumerate(pages):
        a = jnp.exp(log_a - seen)
        seen = seen + page_total
        for c in range(n_groups):
            a_flat[g, :, c * LANES:(c + 1) * LANES] = a[c:c + 1, :]
        a_own = jnp.where(own, jnp.broadcast_to(a_flat[g], (heads, flat)), 0.0).astype(BF16)
        vf = v_refs[g][...].reshape(flat, dh).astype(BF16)
        acc[...] += jnp.dot(a_own, vf, preferred_element_type=F32)
    run[...] = seen

    @pl.when(p == n_steps - 1)
    def _():
        o_ref[...] = acc[...]


def sb_attention_decode(q, k_pool, v_pool, layer, page_table, bias):
    b, heads, dh = q.shape
    n_pages = page_table.shape[1]
    page = k_pool.shape[2]
    n_groups = page * heads // LANES
    group = max(g for g in range(1, DECODE_PAGES_PER_STEP + 1) if n_pages % g == 0)
    n_steps = n_pages // group

    def pool_spec(g):
        def page_map(bi, p, pt):
            return (layer, pt[bi * n_pages + (n_pages - 1 - (p * group + g))], 0, 0, 0)
        return pl.BlockSpec((None, None, page, heads, dh), page_map)

    pool_specs = [pool_spec(g) for g in range(group)]
    q_spec = pl.BlockSpec((None, heads, dh), lambda bi, p, pt: (bi, 0, 0))
    bias_lanes = jnp.tile(bias, LANES // heads).reshape(1, LANES)
    return pl.pallas_call(
        functools.partial(_sb_decode_kernel, scale=dh ** -0.5, n_steps=n_steps, group=group),
        grid_spec=pltpu.PrefetchScalarGridSpec(
            num_scalar_prefetch=1,
            grid=(b, n_steps),
            in_specs=[q_spec, pl.BlockSpec((1, LANES), lambda bi, p, pt: (0, 0))] + pool_specs + pool_specs,
            out_specs=q_spec,
            scratch_shapes=[pltpu.VMEM((heads, dh), F32), pltpu.VMEM((1, LANES), F32),
                            pltpu.VMEM((group, n_groups, LANES), F32), pltpu.VMEM((group, 1, page * heads), F32),
                            pltpu.VMEM((group, n_groups, LANES), F32)],
        ),
        out_shape=jax.ShapeDtypeStruct((b, heads, dh), F32),
        compiler_params=_params(("arbitrary", "arbitrary")),
        name="sb_attention_decode",
    )(page_table.reshape(-1), q, bias_lanes, *([k_pool] * group), *([v_pool] * group))


def _group_norm_gate(o, gate, gn):
    mu = jnp.mean(o, axis=-1, keepdims=True)
    var = jnp.mean((o - mu) ** 2, axis=-1, keepdims=True)
    return (o - mu) * lax.rsqrt(var + EPS) * gn * (gate * jax.nn.sigmoid(gate))


def _ret_prompt_kernel(lg_ref, q_ref, k_ref, v_ref, g_ref, gn_ref, o_ref, s_ref, state, *, chunk, n_chunks):
    lg = lg_ref[pl.program_id(1)]
    n = lax.broadcasted_iota(jnp.int32, (chunk, chunk), 0)
    mcol = lax.broadcasted_iota(jnp.int32, (chunk, chunk), 1)
    diff = (n - mcol).astype(F32)
    decay = jnp.where(diff >= 0, jnp.exp(lg * jnp.maximum(diff, 0.0)), 0.0)
    idx = lax.broadcasted_iota(jnp.int32, (chunk, 1), 0).astype(F32)
    xi = jnp.exp(lg * (idx + 1.0))
    zeta = jnp.exp(lg * (chunk - 1.0 - idx))
    chunk_decay = jnp.exp(lg * jnp.full((1, state.shape[1]), float(chunk), F32))
    gn = gn_ref[...]

    state[...] = jnp.zeros_like(state)

    def body(c, carry):
        rows = pl.ds(pl.multiple_of(c * chunk, chunk), chunk)
        qc, kc, vc = q_ref[rows, :], k_ref[rows, :], v_ref[rows, :]
        s0 = state[...]
        scores = lax.dot_general(qc, kc, (((1,), (1,)), ((), ())), preferred_element_type=F32) * decay
        inner = jnp.dot(scores.astype(BF16), vc, preferred_element_type=F32)
        cross = jnp.dot(qc, s0.astype(BF16), preferred_element_type=F32) * xi
        kz = (kc.astype(F32) * zeta).astype(BF16)
        state[...] = chunk_decay * s0 + lax.dot_general(kz, vc, (((0,), (0,)), ((), ())),
                                                        preferred_element_type=F32)
        o_ref[rows, :] = _group_norm_gate(inner + cross, g_ref[rows, :], gn).astype(o_ref.dtype)
        return carry

    lax.fori_loop(0, n_chunks, body, 0)
    s_ref[...] = state[...]


def retention_prompt(q, k, v, g, gn, log_gamma, batch, seq):
    dk, dv = RET_QK_DIM, RET_V_DIM
    qk_spec = pl.BlockSpec((seq, dk), lambda b, h: (b, h))
    v_spec = pl.BlockSpec((seq, dv), lambda b, h: (b, h))
    return pl.pallas_call(
        functools.partial(_ret_prompt_kernel, chunk=RET_CHUNK, n_chunks=seq // RET_CHUNK),
        grid=(batch, RET_HEADS),
        in_specs=[pl.BlockSpec(memory_space=pltpu.SMEM), qk_spec, qk_spec, v_spec, v_spec,
                  pl.BlockSpec((1, dv), lambda b, h: (0, h))],
        out_specs=[v_spec, pl.BlockSpec((None, None, dk, dv), lambda b, h: (b, h, 0, 0))],
        out_shape=[jax.ShapeDtypeStruct(v.shape, BF16),
                   jax.ShapeDtypeStruct((batch, RET_HEADS, dk, dv), F32)],
        scratch_shapes=[pltpu.VMEM((dk, dv), F32)],
        compiler_params=_params(("arbitrary", "arbitrary")),
        name="retention_prompt",
    )(log_gamma, q, k, v, g, gn.reshape(1, -1))


def _ret_decode_kernel(lg_ref, q_ref, k_ref, v_ref, g_ref, gn_ref, s0_ref, o_ref, s_ref):
    lg = lg_ref[pl.program_id(1)]
    dv = v_ref.shape[-1]
    gamma = jnp.exp(lg * jnp.ones((1, dv), F32))
    q, k, v, s0 = q_ref[...], k_ref[...], v_ref[...], s0_ref[...]
    qk = jnp.sum(q * k, axis=0, keepdims=True)
    cross = jnp.sum(q * s0, axis=0, keepdims=True) * gamma
    o = qk * v + cross
    s_ref[...] = gamma * s0 + k * v
    o_ref[...] = _group_norm_gate(o, g_ref[...], gn_ref[...]).astype(o_ref.dtype)


def retention_decode(q, k, v, g, gn, log_gamma, state):
    b = q.shape[0]
    h, dk, dv = RET_HEADS, RET_QK_DIM, RET_V_DIM
    col_spec = pl.BlockSpec((None, None, dk, 1), lambda bi, hi: (bi, hi, 0, 0))
    row_spec = pl.BlockSpec((None, None, 1, dv), lambda bi, hi: (bi, hi, 0, 0))
    st_spec = pl.BlockSpec((None, None, dk, dv), lambda bi, hi: (bi, hi, 0, 0))
    o, s_new = pl.pallas_call(
        _ret_decode_kernel,
        grid=(b, h),
        in_specs=[pl.BlockSpec(memory_space=pltpu.SMEM), col_spec, col_spec, row_spec, row_spec,
                  pl.BlockSpec((None, 1, dv), lambda bi, hi: (hi, 0, 0)), st_spec],
        out_specs=[row_spec, st_spec],
        out_shape=[jax.ShapeDtypeStruct((b, h, 1, dv), BF16), jax.ShapeDtypeStruct(state.shape, F32)],
        compiler_params=_params(("arbitrary", "arbitrary")),
        name="retention_decode",
    )(log_gamma, q.reshape(b, h, dk, 1), k.reshape(b, h, dk, 1), v.reshape(b, h, 1, dv),
      g.reshape(b, h, 1, dv), gn.reshape(h, 1, dv), state)
    return o.reshape(b, h * dv), s_new


def _stack(xs):
    return xs[0][None] if len(xs) == 1 else jnp.stack(xs)


def _rope_tables(pos, rows):
    half = RET_QK_DIM // 2
    inv = ROPE_BASE ** (-jnp.arange(half, dtype=F32) / half)
    ang = pos.astype(F32)[:, None] * inv[None, :]
    cos, sin = jnp.cos(ang), jnp.sin(ang)
    reps = rows // pos.shape[0]
    return jnp.tile(cos, (reps, 1)), jnp.tile(sin, (reps, 1))


def kernel(x_prompt, x_sample, cache_k, cache_v, state_ret, page_table, norm_g, ffn_w_gate, ffn_w_up, ffn_w_down,
           sb_w_qkv, sb_w_o, sb_bias, ret_w_qkvg, ret_gn_g, ret_w_o):
    batch, seq, d = x_prompt.shape
    dec_batch, dec_seq, _ = x_sample.shape
    depth = norm_g.shape[0]
    past_len = page_table.shape[1] * PAGE_SIZE
    assert dec_seq == 1, "the decode kernels take one new token per sequence"
    mp, ms = batch * seq, dec_batch * dec_seq

    log_gamma = jnp.log1p(-jnp.exp2(-5.0 - jnp.arange(RET_HEADS, dtype=F32)))
    rope_p = _rope_tables(jnp.arange(seq, dtype=jnp.int32), mp)
    rope_s = _rope_tables(past_len + jnp.arange(dec_seq, dtype=jnp.int32), ms)

    w_down = ffn_w_down.astype(BF16)
    w_sb_o = sb_w_o.astype(BF16)[None]
    w_ret_o = ret_w_o.astype(BF16)[None]

    xp, xs = x_prompt.reshape(mp, d), x_sample.reshape(ms, d)
    hp, hs = prenorm(xp, norm_g[0, 0]), prenorm(xs, norm_g[0, 0])
    kp, vp, ksm, vsm, rsp, rss = [], [], [], [], [], []

    for i in range(depth):
        g = norm_g[i]
        layer = i // 2

        def ffn(x, h, j, g_post, g_next):
            act = swiglu_up(h, ffn_w_gate, ffn_w_up, (i, j))
            return down_residual(act, w_down, (i, j), x, g_post, g_next, 0.5)

        xp, hp = ffn(xp, hp, 0, g[1], g[2])
        xs, hs = ffn(xs, hs, 0, g[1], g[2])

        if i % 2 == 0:
            w = sb_w_qkv[layer]
            bias = sb_bias[layer]
            q, k, v = project(hp, w, [(0, BF16, None), (d, F32, None), (2 * d, F32, None)], d, QKV_COL_TILE)
            mix_p = sb_attention_prompt(q, k, v, bias, batch, seq)
            kp.append(k.reshape(batch, seq, SB_HEADS, SB_HEAD_DIM))
            vp.append(v.reshape(batch, seq, SB_HEADS, SB_HEAD_DIM))

            q_s, k_s, v_s = project(hs, w, [(0, F32, None), (d, F32, None), (2 * d, F32, None)], d, QKV_COL_TILE)
            mix_s = sb_attention_decode(q_s.reshape(ms, SB_HEADS, SB_HEAD_DIM), cache_k, cache_v, layer,
                                        page_table, bias).reshape(ms, d)
            ksm.append(k_s.reshape(dec_batch, dec_seq, SB_HEADS, SB_HEAD_DIM))
            vsm.append(v_s.reshape(dec_batch, dec_seq, SB_HEADS, SB_HEAD_DIM))
            w_o = (w_sb_o, (0, layer))
        else:
            w = ret_w_qkvg[layer]
            gn = ret_gn_g[layer]
            q, k = project(hp, w, [(0, BF16, RET_QK_DIM ** -0.5), (d, BF16, 1.0)], d, COL_TILE, rope=rope_p)
            v, gate = project(hp, w, [(2 * d, BF16, None), (4 * d, F32, None)], 2 * d, COL_TILE)
            mix_p, st_p = retention_prompt(q, k, v, gate, gn, log_gamma, batch, seq)
            rsp.append(st_p)

            q_s, k_s = project(hs, w, [(0, F32, RET_QK_DIM ** -0.5), (d, F32, 1.0)], d, COL_TILE, rope=rope_s)
            v_s, gate_s = project(hs, w, [(2 * d, F32, None), (4 * d, F32, None)], 2 * d, COL_TILE)
            mix_s, st_s = retention_decode(q_s, k_s, v_s, gate_s, gn, log_gamma, state_ret[layer])
            rss.append(st_s)
            w_o = (w_ret_o, (0, layer))

        xp, hp = down_residual(mix_p, w_o[0], w_o[1], xp, g[3], g[4], 1.0)
        xs, hs = down_residual(mix_s, w_o[0], w_o[1], xs, g[3], g[4], 1.0)

        g_next = norm_g[i + 1, 0] if i + 1 < depth else None
        xp, hp = ffn(xp, hp, 1, g[5], g_next)
        xs, hs = ffn(xs, hs, 1, g[5], g_next)

    return (xp.reshape(batch, seq, d), xs.reshape(dec_batch, dec_seq, d), _stack(kp), _stack(vp),
            _stack(ksm), _stack(vsm), _stack(rsp), _stack(rss))
```

```python
import functools

import jax
import jax.numpy as jnp
from jax import lax
from jax.experimental import pallas as pl
from jax.experimental.pallas import tpu as pltpu

F32 = jnp.float32
BF16 = jnp.bfloat16

EPS = 1e-6
SB_HEADS = 16
SB_HEAD_DIM = 128
PAGE_SIZE = 128
RET_HEADS = 8
RET_QK_DIM = 256
RET_V_DIM = 512
RET_CHUNK = 128
ROPE_BASE = 10000.0

VMEM_LIMIT_BYTES = 60 * 1024 * 1024
LANES = 128

ROW_TILE = 1024
DOWN_ROW_TILE = 1024
DOWN_ROW_SPLITS = 2
COL_TILE = 512
QKV_COL_TILE = 256
K_TILE = 512
ATT_BLOCK = 256
DECODE_PAGES_PER_STEP = 4


def _params(sem):
    return pltpu.CompilerParams(dimension_semantics=sem, vmem_limit_bytes=VMEM_LIMIT_BYTES)


def _rms(x, g):
    return x * lax.rsqrt(jnp.mean(x * x, axis=-1, keepdims=True) + EPS) * g


def _row_tile(m, want):
    return want if m % want == 0 else m


def _prenorm_kernel(x_ref, g_ref, o_ref):
    o_ref[...] = _rms(x_ref[...], g_ref[...]).astype(o_ref.dtype)


def prenorm(x, g):
    m, d = x.shape
    tm = _row_tile(m, DOWN_ROW_TILE)
    return pl.pallas_call(
        _prenorm_kernel,
        grid=(m // tm,),
        in_specs=[pl.BlockSpec((tm, d), lambda i: (i, 0)), pl.BlockSpec((1, d), lambda i: (0, 0))],
        out_specs=pl.BlockSpec((tm, d), lambda i: (i, 0)),
        out_shape=jax.ShapeDtypeStruct((m, d), BF16),
        compiler_params=_params(("arbitrary",)),
        name="prenorm",
    )(x, g.reshape(1, d))


def _swiglu_kernel(*refs, has_copy):
    if has_copy:
        a_ref, as_ref, wg_ref, wu_ref, wsrc_ref, o_ref, os_ref, wcopy_ref, wg_bf, wu_bf = refs
    else:
        a_ref, as_ref, wg_ref, wu_ref, o_ref, os_ref, wg_bf, wu_bf = refs
    first = pl.program_id(1) == 0
    tm = a_ref.shape[0]

    def act(a):
        gate = jnp.dot(a, wg_bf[...], preferred_element_type=F32)
        up = jnp.dot(a, wu_bf[...], preferred_element_type=F32)
        return (gate * jax.nn.sigmoid(gate) * up).astype(o_ref.dtype)

    @pl.when(first)
    def _():
        wg_bf[...] = wg_ref[...].astype(BF16)
        wu_bf[...] = wu_ref[...].astype(BF16)
        if has_copy:
            wcopy_ref[...] = wsrc_ref[...].astype(BF16)
        y = act(jnp.concatenate([a_ref[...], as_ref[...]], axis=0))
        o_ref[...] = y[:tm]
        os_ref[...] = y[tm:]

    @pl.when(jnp.logical_not(first))
    def _():
        o_ref[...] = act(a_ref[...])


def swiglu_up(h, hs, w_gate, w_up, widx, w_copy_src=None):
    m, k = h.shape
    n = w_gate.shape[-1]
    tm, tn = _row_tile(m, ROW_TILE), COL_TILE
    i0, i1 = widx
    wspec = pl.BlockSpec((None, None, k, tn), lambda j, i: (i0, i1, 0, j))
    in_specs = [pl.BlockSpec((tm, k), lambda j, i: (i, 0)), pl.BlockSpec(hs.shape, lambda j, i: (0, 0)), wspec, wspec]
    out_specs = [pl.BlockSpec((tm, tn), lambda j, i: (i, j)), pl.BlockSpec((hs.shape[0], tn), lambda j, i: (0, j))]
    out_shape = [jax.ShapeDtypeStruct((m, n), BF16), jax.ShapeDtypeStruct((hs.shape[0], n), BF16)]
    args = [h, hs, w_gate, w_up]
    if w_copy_src is not None:
        d = w_copy_src.shape[-1]
        in_specs.append(pl.BlockSpec((None, None, tn, d), lambda j, i: (i0, i1, j, 0)))
        out_specs.append(pl.BlockSpec((tn, d), lambda j, i: (j, 0)))
        out_shape.append(jax.ShapeDtypeStruct((n, d), BF16))
        args.append(w_copy_src)
    return pl.pallas_call(
        functools.partial(_swiglu_kernel, has_copy=w_copy_src is not None),
        grid=(n // tn, m // tm),
        in_specs=in_specs,
        out_specs=out_specs,
        out_shape=out_shape,
        scratch_shapes=[pltpu.VMEM((k, tn), BF16), pltpu.VMEM((k, tn), BF16)],
        compiler_params=_params(("arbitrary", "arbitrary")),
        name="swiglu_up",
    )(*args)


def _proj_kernel(*refs, n_out, ropes, has_copy):
    has_rope = any(r is not None for r in ropes)
    refs = list(refs)
    a_ref, as_ref = refs[:2]
    w_refs = refs[2:2 + n_out]
    pos = 2 + n_out
    if has_rope:
        cos_ref, sin_ref, cos_s_ref, sin_s_ref = refs[pos:pos + 4]
        pos += 4
    if has_copy:
        wsrc_ref = refs[pos]
        pos += 1
    o_refs = refs[pos:pos + n_out]
    os_refs = refs[pos + n_out:pos + 2 * n_out]
    pos += 2 * n_out
    if has_copy:
        wcopy_ref = refs[pos]
        pos += 1
    w_bfs = refs[pos:]
    first = pl.program_id(1) == 0
    tm = a_ref.shape[0]

    def outputs(a, cos, sin):
        ys = []
        for w_bf, rope_scale in zip(w_bfs, ropes):
            y = jnp.dot(a, w_bf[...], preferred_element_type=F32)
            if rope_scale is not None:
                half = cos.shape[-1]
                parts = []
                for hd in range(y.shape[-1] // (2 * half)):
                    x1 = y[:, hd * 2 * half:hd * 2 * half + half]
                    x2 = y[:, hd * 2 * half + half:(hd + 1) * 2 * half]
                    parts += [(x1 * cos - x2 * sin) * rope_scale, (x1 * sin + x2 * cos) * rope_scale]
                y = jnp.concatenate(parts, axis=1)
            ys.append(y)
        return ys

    @pl.when(first)
    def _():
        for w_ref, w_bf in zip(w_refs, w_bfs):
            w_bf[...] = w_ref[...].astype(BF16)
        if has_copy:
            wcopy_ref[...] = wsrc_ref[...].astype(BF16)
        cos = jnp.concatenate([cos_ref[...], cos_s_ref[...]], axis=0) if has_rope else None
        sin = jnp.concatenate([sin_ref[...], sin_s_ref[...]], axis=0) if has_rope else None
        ys = outputs(jnp.concatenate([a_ref[...], as_ref[...]], axis=0), cos, sin)
        for y, o_ref, os_ref in zip(ys, o_refs, os_refs):
            o_ref[...] = y[:tm].astype(o_ref.dtype)
            os_ref[...] = y[tm:].astype(os_ref.dtype)

    @pl.when(jnp.logical_not(first))
    def _():
        ys = outputs(a_ref[...], cos_ref[...] if has_rope else None, sin_ref[...] if has_rope else None)
        for y, o_ref in zip(ys, o_refs):
            o_ref[...] = y.astype(o_ref.dtype)


def project(h, hs, w, outs, ncols, tn, rope=None, rope_s=None, w_copy_src=None):
    m, k = h.shape
    ms = hs.shape[0]
    tm = _row_tile(m, ROW_TILE)
    ncol_steps = ncols // tn
    ropes = tuple(o[2] for o in outs)
    in_specs = [pl.BlockSpec((tm, k), lambda j, i: (i, 0)), pl.BlockSpec((ms, k), lambda j, i: (0, 0))]
    for c0, _, _ in outs:
        in_specs.append(pl.BlockSpec((k, tn), functools.partial(lambda j, i, off: (0, off + j), off=c0 // tn)))
    args = [h, hs] + [w] * len(outs)
    if rope is not None:
        half = rope[0].shape[-1]
        in_specs += ([pl.BlockSpec((tm, half), lambda j, i: (i, 0))] * 2
                     + [pl.BlockSpec((ms, half), lambda j, i: (0, 0))] * 2)
        args += list(rope) + list(rope_s)
    out_specs = ([pl.BlockSpec((tm, tn), lambda j, i: (i, j))] * len(outs)
                 + [pl.BlockSpec((ms, tn), lambda j, i: (0, j))] * len(outs))
    out_shape = ([jax.ShapeDtypeStruct((m, ncols), o[1]) for o in outs]
                 + [jax.ShapeDtypeStruct((ms, ncols), F32) for _ in outs])
    if w_copy_src is not None:
        r, d = w_copy_src.shape
        slab = r // ncol_steps
        in_specs.append(pl.BlockSpec((slab, d), lambda j, i: (j, 0)))
        args.append(w_copy_src)
        out_specs.append(pl.BlockSpec((slab, d), lambda j, i: (j, 0)))
        out_shape.append(jax.ShapeDtypeStruct((r, d), BF16))
    res = pl.pallas_call(
        functools.partial(_proj_kernel, n_out=len(outs), ropes=ropes, has_copy=w_copy_src is not None),
        grid=(ncol_steps, m // tm),
        in_specs=in_specs,
        out_specs=out_specs,
        out_shape=out_shape,
        scratch_shapes=[pltpu.VMEM((k, tn), BF16)] * len(outs),
        compiler_params=_params(("arbitrary", "arbitrary")),
        name="project_rope" if rope is not None else "project",
    )(*args)
    n = len(outs)
    return res[:n], res[n:2 * n], (res[2 * n] if w_copy_src is not None else None)


def _down_kernel(*refs, res_scale, nk, has_next, row_splits):
    if has_next:
        a_ref, as_ref, w_ref, x_ref, xs_ref, gpost_ref, gnext_ref, xo_ref, xso_ref, ho_ref, hso_ref = refs
    else:
        a_ref, as_ref, w_ref, x_ref, xs_ref, gpost_ref, xo_ref, xso_ref = refs
    first_rows = pl.program_id(0) == 0
    kk = pl.program_id(1)
    tm = xo_ref.shape[0]
    rows = tm // row_splits
    last = slice(tm - rows, tm)

    @pl.when(kk == 0)
    def _():
        xo_ref[...] = jnp.zeros_like(xo_ref)

    @pl.when(jnp.logical_and(kk == 0, first_rows))
    def _():
        xso_ref[...] = jnp.zeros_like(xso_ref)

    w = w_ref[...]
    for r in range(row_splits - 1):
        sl = slice(r * rows, (r + 1) * rows)
        xo_ref[sl, :] += jnp.dot(a_ref[sl, :].astype(BF16), w, preferred_element_type=F32)

    @pl.when(first_rows)
    def _():
        a = jnp.concatenate([a_ref[last, :].astype(BF16), as_ref[...].astype(BF16)], axis=0)
        d = jnp.dot(a, w, preferred_element_type=F32)
        xo_ref[last, :] += d[:rows]
        xso_ref[...] += d[rows:]

    @pl.when(jnp.logical_not(first_rows))
    def _():
        xo_ref[last, :] += jnp.dot(a_ref[last, :].astype(BF16), w, preferred_element_type=F32)

    def finish(x, acc_ref, sl, h_ref):
        xn = x + res_scale * _rms(acc_ref[sl, :], gpost_ref[...])
        acc_ref[sl, :] = xn
        if has_next:
            h_ref[sl, :] = _rms(xn, gnext_ref[...]).astype(h_ref.dtype)

    @pl.when(kk == nk - 1)
    def _():
        for r in range(row_splits):
            sl = slice(r * rows, (r + 1) * rows)
            finish(x_ref[sl, :], xo_ref, sl, ho_ref if has_next else None)

    @pl.when(jnp.logical_and(kk == nk - 1, first_rows))
    def _():
        finish(xs_ref[...], xso_ref, slice(None), hso_ref if has_next else None)


def down_residual(a, a_s, w, widx, x, x_s, g_post, g_next, res_scale):
    m, k = a.shape
    ms = a_s.shape[0]
    d = w.shape[-1]
    tm, tk = _row_tile(m, DOWN_ROW_TILE), K_TILE
    row_splits = DOWN_ROW_SPLITS if tm % (16 * DOWN_ROW_SPLITS) == 0 else 1
    nk = k // tk
    i0, i1 = widx
    has_next = g_next is not None
    row = pl.BlockSpec((tm, d), lambda i, kk: (i, 0))
    srow = pl.BlockSpec((ms, d), lambda i, kk: (0, 0))
    vec = pl.BlockSpec((1, d), lambda i, kk: (0, 0))
    in_specs = [pl.BlockSpec((tm, tk), lambda i, kk: (i, kk)), pl.BlockSpec((ms, tk), lambda i, kk: (0, kk)),
                pl.BlockSpec((None, None, tk, d), lambda i, kk: (i0, i1, kk, 0)), row, srow, vec]
    args = [a, a_s, w, x, x_s, g_post.reshape(1, d)]
    out_specs = [row, srow]
    out_shape = [jax.ShapeDtypeStruct((m, d), F32), jax.ShapeDtypeStruct((ms, d), F32)]
    if has_next:
        in_specs.append(vec)
        args.append(g_next.reshape(1, d))
        out_specs += [row, srow]
        out_shape += [jax.ShapeDtypeStruct((m, d), BF16), jax.ShapeDtypeStruct((ms, d), BF16)]
    outs = pl.pallas_call(
        functools.partial(_down_kernel, res_scale=res_scale, nk=nk, has_next=has_next, row_splits=row_splits),
        grid=(m // tm, nk),
        in_specs=in_specs,
        out_specs=out_specs,
        out_shape=out_shape,
        compiler_params=_params(("arbitrary", "arbitrary")),
        name="down_residual",
    )(*args)
    return tuple(outs) if has_next else (outs[0], outs[1], None, None)


def _softplus(z):
    return jnp.maximum(z, 0.0) + jnp.log(1.0 + jnp.exp(-jnp.abs(z)))


def _suffix_sum(sp, tri):
    hi = sp.astype(BF16)
    lo = (sp - hi.astype(F32)).astype(BF16)
    return jnp.dot(hi, tri, preferred_element_type=F32) + jnp.dot(lo, tri, preferred_element_type=F32)


def _sb_prompt_kernel(bias_ref, q_ref, k_ref, v_ref, o_ref, acc, run, *, blk, scale):
    nblk = q_ref.shape[0] // blk
    bias = bias_ref[pl.program_id(1)]
    row = lax.broadcasted_iota(jnp.int32, (blk, blk), 0)
    col = lax.broadcasted_iota(jnp.int32, (blk, blk), 1)
    tri = (row >= col).astype(BF16)
    visible = col < row
    k_bf = k_ref[...].astype(BF16)
    v_bf = v_ref[...].astype(BF16)

    def mask_diagonal(x):
        top = jnp.where(visible, x[:blk], 0.0)
        return top if x.shape[0] == blk else jnp.concatenate([top, x[blk:]], axis=0)

    acc[...] = jnp.zeros_like(acc)
    run[...] = jnp.zeros_like(run)

    for j in reversed(range(nblk)):
        r0 = j * blk
        z = lax.dot_general(q_ref[r0:, :], k_bf[r0:r0 + blk], (((1,), (1,)), ((), ())),
                            preferred_element_type=F32) * scale + bias
        local = _suffix_sum(mask_diagonal(_softplus(z)), tri)
        p = mask_diagonal(jnp.exp(z - local))
        pv = jnp.dot(p.astype(BF16), v_bf[r0:r0 + blk], preferred_element_type=F32)
        acc[r0:, :] += jnp.exp(-run[r0:, :]) * pv
        run[r0:, :] += local[:, 0:1]

    o_ref[...] = acc[...].astype(o_ref.dtype)


def sb_attention_prompt(q, k, v, bias, batch, seq):
    dh = SB_HEAD_DIM
    spec = pl.BlockSpec((seq, dh), lambda b, h: (b, h))
    return pl.pallas_call(
        functools.partial(_sb_prompt_kernel, blk=ATT_BLOCK, scale=dh ** -0.5),
        grid=(batch, SB_HEADS),
        in_specs=[pl.BlockSpec(memory_space=pltpu.SMEM), spec, spec, spec],
        out_specs=spec,
        out_shape=jax.ShapeDtypeStruct(q.shape, BF16),
        scratch_shapes=[pltpu.VMEM((seq, dh), F32), pltpu.VMEM((seq, 1), F32)],
        compiler_params=_params(("arbitrary", "arbitrary")),
        name="sb_attention_prompt",
    )(bias, q, k, v)


def _sb_decode_kernel(pt_ref, q_ref, bias_ref, *refs, scale, n_steps, group):
    del pt_ref
    k_refs, v_refs = refs[:group], refs[group:2 * group]
    o_ref, acc, run, z_rows, a_flat, g_rows = refs[2 * group:]
    p = pl.program_id(1)
    page, heads, dh = k_refs[0].shape
    flat = page * heads
    n_groups = flat // LANES

    @pl.when(p == 0)
    def _():
        acc[...] = jnp.zeros_like(acc)
        run[...] = jnp.zeros_like(run)

    lane = lax.broadcasted_iota(jnp.int32, (heads, flat), 1)
    own = lax.rem(lane, heads) == lax.broadcasted_iota(jnp.int32, (heads, flat), 0)
    src = lax.broadcasted_iota(jnp.int32, (LANES, 2 * LANES), 0)
    dst = lax.broadcasted_iota(jnp.int32, (LANES, 2 * LANES), 1)
    same_head = lax.rem(src, heads) == lax.rem(dst, heads)
    newer = (src // heads) >= (lax.rem(dst, LANES) // heads)
    sel = (same_head & (newer | (dst >= LANES))).astype(BF16)
    q = q_ref[...].astype(BF16)
    bias = bias_ref[...]

    def logits(g):
        kf = k_refs[g][...].reshape(flat, dh).astype(BF16)
        zt = lax.dot_general(q, kf, (((1,), (1,)), ((), ())), preferred_element_type=F32)
        zf = jnp.sum(jnp.where(own, zt, 0.0), axis=0, keepdims=True)
        for c in range(n_groups):
            z_rows[g, c:c + 1, :] = zf[:, c * LANES:(c + 1) * LANES]
        z = z_rows[g] * scale + bias
        both = _suffix_sum(_softplus(z), sel)
        local, group_total = both[:, :LANES], both[:, LANES:]
        newer_groups = jnp.zeros((1, LANES), F32)
        for c in range(n_groups - 1, -1, -1):
            g_rows[g, c:c + 1, :] = newer_groups
            newer_groups = newer_groups + group_total[c:c + 1, :]
        return z - (local + g_rows[g]), newer_groups

    pages = [logits(g) for g in range(group)]
    seen = run[...]
    for g, (log_a, page_total) in enumerate(pages):
        a = jnp.exp(log_a - seen)
        seen = seen + page_total
        for c in range(n_groups):
            a_flat[g, :, c * LANES:(c + 1) * LANES] = a[c:c + 1, :]
        a_own = jnp.where(own, jnp.broadcast_to(a_flat[g], (heads, flat)), 0.0).astype(BF16)
        vf = v_refs[g][...].reshape(flat, dh).astype(BF16)
        acc[...] += jnp.dot(a_own, vf, preferred_element_type=F32)
    run[...] = seen

    @pl.when(p == n_steps - 1)
    def _():
        o_ref[...] = acc[...]


def sb_attention_decode(q, k_pool, v_pool, layer, page_table, bias):
    b, heads, dh = q.shape
    n_pages = page_table.shape[1]
    page = k_pool.shape[2]
    n_groups = page * heads // LANES
    group = max(g for g in range(1, DECODE_PAGES_PER_STEP + 1) if n_pages % g == 0)
    n_steps = n_pages // group

    def pool_spec(g):
        def page_map(bi, p, pt):
            return (layer, pt[bi * n_pages + (n_pages - 1 - (p * group + g))], 0, 0, 0)
        return pl.BlockSpec((None, None, page, heads, dh), page_map)

    pool_specs = [pool_spec(g) for g in range(group)]
    q_spec = pl.BlockSpec((None, heads, dh), lambda bi, p, pt: (bi, 0, 0))
    bias_lanes = jnp.tile(bias, LANES // heads).reshape(1, LANES)
    return pl.pallas_call(
        functools.partial(_sb_decode_kernel, scale=dh ** -0.5, n_steps=n_steps, group=group),
        grid_spec=pltpu.PrefetchScalarGridSpec(
            num_scalar_prefetch=1,
            grid=(b, n_steps),
            in_specs=[q_spec, pl.BlockSpec((1, LANES), lambda bi, p, pt: (0, 0))] + pool_specs + pool_specs,
            out_specs=q_spec,
            scratch_shapes=[pltpu.VMEM((heads, dh), F32), pltpu.VMEM((1, LANES), F32),
                            pltpu.VMEM((group, n_groups, LANES), F32), pltpu.VMEM((group, 1, page * heads), F32),
                            pltpu.VMEM((group, n_groups, LANES), F32)],
        ),
        out_shape=jax.ShapeDtypeStruct((b, heads, dh), F32),
        compiler_params=_params(("arbitrary", "arbitrary")),
        name="sb_attention_decode",
    )(page_table.reshape(-1), q, bias_lanes, *([k_pool] * group), *([v_pool] * group))


def _group_norm_gate(o, gate, gn):
    mu = jnp.mean(o, axis=-1, keepdims=True)
    var = jnp.mean((o - mu) ** 2, axis=-1, keepdims=True)
    return (o - mu) * lax.rsqrt(var + EPS) * gn * (gate * jax.nn.sigmoid(gate))


def _ret_prompt_kernel(lg_ref, q_ref, k_ref, v_ref, g_ref, gn_ref, o_ref, s_ref, state, *, chunk, n_chunks):
    lg = lg_ref[pl.program_id(1)]
    n = lax.broadcasted_iota(jnp.int32, (chunk, chunk), 0)
    mcol = lax.broadcasted_iota(jnp.int32, (chunk, chunk), 1)
    diff = (n - mcol).astype(F32)
    decay = jnp.where(diff >= 0, jnp.exp(lg * jnp.maximum(diff, 0.0)), 0.0)
    idx = lax.broadcasted_iota(jnp.int32, (chunk, 1), 0).astype(F32)
    xi = jnp.exp(lg * (idx + 1.0))
    zeta = jnp.exp(lg * (chunk - 1.0 - idx))
    chunk_decay = jnp.exp(lg * jnp.full((1, state.shape[1]), float(chunk), F32))
    gn = gn_ref[...]

    state[...] = jnp.zeros_like(state)

    for c in range(n_chunks):
        rows = slice(c * chunk, (c + 1) * chunk)
        qc, kc, vc = q_ref[rows, :], k_ref[rows, :], v_ref[rows, :]
        s0 = state[...]
        scores = lax.dot_general(qc, kc, (((1,), (1,)), ((), ())), preferred_element_type=F32) * decay
        inner = jnp.dot(scores.astype(BF16), vc, preferred_element_type=F32)
        cross = jnp.dot(qc, s0.astype(BF16), preferred_element_type=F32) * xi
        kz = (kc.astype(F32) * zeta).astype(BF16)
        state[...] = chunk_decay * s0 + lax.dot_general(kz, vc, (((0,), (0,)), ((), ())),
                                                        preferred_element_type=F32)
        o_ref[rows, :] = _group_norm_gate(inner + cross, g_ref[rows, :], gn).astype(o_ref.dtype)
    s_ref[...] = state[...]


def retention_prompt(q, k, v, g, gn, log_gamma, batch, seq):
    dk, dv = RET_QK_DIM, RET_V_DIM
    qk_spec = pl.BlockSpec((seq, dk), lambda b, h: (b, h))
    v_spec = pl.BlockSpec((seq, dv), lambda b, h: (b, h))
    return pl.pallas_call(
        functools.partial(_ret_prompt_kernel, chunk=RET_CHUNK, n_chunks=seq // RET_CHUNK),
        grid=(batch, RET_HEADS),
        in_specs=[pl.BlockSpec(memory_space=pltpu.SMEM), qk_spec, qk_spec, v_spec, v_spec,
                  pl.BlockSpec((1, dv), lambda b, h: (0, h))],
        out_specs=[v_spec, pl.BlockSpec((None, None, dk, dv), lambda b, h: (b, h, 0, 0))],
        out_shape=[jax.ShapeDtypeStruct(v.shape, BF16),
                   jax.ShapeDtypeStruct((batch, RET_HEADS, dk, dv), F32)],
        scratch_shapes=[pltpu.VMEM((dk, dv), F32)],
        compiler_params=_params(("arbitrary", "arbitrary")),
        name="retention_prompt",
    )(log_gamma, q, k, v, g, gn.reshape(1, -1))


def _ret_decode_kernel(lg_ref, q_ref, k_ref, v_ref, g_ref, gn_ref, s0_ref, o_ref, s_ref):
    lg = lg_ref[pl.program_id(1)]
    dv = v_ref.shape[-1]
    gamma = jnp.exp(lg * jnp.ones((1, dv), F32))
    q, k, v, s0 = q_ref[...], k_ref[...], v_ref[...], s0_ref[...]
    qk = jnp.sum(q * k, axis=0, keepdims=True)
    cross = jnp.sum(q * s0, axis=0, keepdims=True) * gamma
    o = qk * v + cross
    s_ref[...] = gamma * s0 + k * v
    o_ref[...] = _group_norm_gate(o, g_ref[...], gn_ref[...]).astype(o_ref.dtype)


def retention_decode(q, k, v, g, gn, log_gamma, state):
    b = q.shape[0]
    h, dk, dv = RET_HEADS, RET_QK_DIM, RET_V_DIM
    col_spec = pl.BlockSpec((None, None, dk, 1), lambda bi, hi: (bi, hi, 0, 0))
    row_spec = pl.BlockSpec((None, None, 1, dv), lambda bi, hi: (bi, hi, 0, 0))
    st_spec = pl.BlockSpec((None, None, dk, dv), lambda bi, hi: (bi, hi, 0, 0))
    o, s_new = pl.pallas_call(
        _ret_decode_kernel,
        grid=(b, h),
        in_specs=[pl.BlockSpec(memory_space=pltpu.SMEM), col_spec, col_spec, row_spec, row_spec,
                  pl.BlockSpec((None, 1, dv), lambda bi, hi: (hi, 0, 0)), st_spec],
        out_specs=[row_spec, st_spec],
        out_shape=[jax.ShapeDtypeStruct((b, h, 1, dv), BF16), jax.ShapeDtypeStruct(state.shape, F32)],
        compiler_params=_params(("arbitrary", "arbitrary")),
        name="retention_decode",
    )(log_gamma, q.reshape(b, h, dk, 1), k.reshape(b, h, dk, 1), v.reshape(b, h, 1, dv),
      g.reshape(b, h, 1, dv), gn.reshape(h, 1, dv), state)
    return o.reshape(b, h * dv), s_new


def _stack(xs):
    return xs[0][None] if len(xs) == 1 else jnp.stack(xs)


def _rope_tables(pos, rows):
    half = RET_QK_DIM // 2
    inv = ROPE_BASE ** (-jnp.arange(half, dtype=F32) / half)
    ang = pos.astype(F32)[:, None] * inv[None, :]
    cos, sin = jnp.cos(ang), jnp.sin(ang)
    reps = rows // pos.shape[0]
    return jnp.tile(cos, (reps, 1)), jnp.tile(sin, (reps, 1))


SAMPLE_ROWS = 16


def kernel(x_prompt, x_sample, cache_k, cache_v, state_ret, page_table, norm_g, ffn_w_gate, ffn_w_up, ffn_w_down,
           sb_w_qkv, sb_w_o, sb_bias, ret_w_qkvg, ret_gn_g, ret_w_o):
    batch, seq, d = x_prompt.shape
    dec_batch, dec_seq, _ = x_sample.shape
    depth = norm_g.shape[0]
    past_len = page_table.shape[1] * PAGE_SIZE
    assert dec_seq == 1, "the decode kernels take one new token per sequence"
    mp, ms = batch * seq, dec_batch * dec_seq
    assert ms <= SAMPLE_ROWS

    def pad_rows(t):
        return jnp.concatenate([t, jnp.zeros((SAMPLE_ROWS - ms, t.shape[1]), t.dtype)], axis=0)

    log_gamma = jnp.log1p(-jnp.exp2(-5.0 - jnp.arange(RET_HEADS, dtype=F32)))
    rope_p = _rope_tables(jnp.arange(seq, dtype=jnp.int32), mp)
    rope_s = _rope_tables(past_len + jnp.arange(dec_seq, dtype=jnp.int32), ms)
    rope_s = (pad_rows(rope_s[0]), pad_rows(rope_s[1]))

    xp, xs = x_prompt.reshape(mp, d), pad_rows(x_sample.reshape(ms, d))
    hp, hs = prenorm(xp, norm_g[0, 0]), prenorm(xs, norm_g[0, 0])
    kp, vp, ksm, vsm, rsp, rss = [], [], [], [], [], []

    for i in range(depth):
        g = norm_g[i]
        layer = i // 2

        def ffn(x, x_s, h, h_s, j, g_post, g_next):
            act, act_s, w_down = swiglu_up(h, h_s, ffn_w_gate, ffn_w_up, (i, j), w_copy_src=ffn_w_down)
            return down_residual(act, act_s, w_down[None, None], (0, 0), x, x_s, g_post, g_next, 0.5)

        xp, xs, hp, hs = ffn(xp, xs, hp, hs, 0, g[1], g[2])

        if i % 2 == 0:
            w = sb_w_qkv[layer]
            bias = sb_bias[layer]
            (q, k, v), (q_s, k_s, v_s), w_o = project(
                hp, hs, w, [(0, BF16, None), (d, F32, None), (2 * d, F32, None)], d, QKV_COL_TILE,
                w_copy_src=sb_w_o[layer])
            mix_p = sb_attention_prompt(q, k, v, bias, batch, seq)
            kp.append(k.reshape(batch, seq, SB_HEADS, SB_HEAD_DIM))
            vp.append(v.reshape(batch, seq, SB_HEADS, SB_HEAD_DIM))
            mix_s = sb_attention_decode(q_s[:ms].reshape(ms, SB_HEADS, SB_HEAD_DIM), cache_k, cache_v, layer,
                                        page_table, bias).reshape(ms, d)
            ksm.append(k_s[:ms].reshape(dec_batch, dec_seq, SB_HEADS, SB_HEAD_DIM))
            vsm.append(v_s[:ms].reshape(dec_batch, dec_seq, SB_HEADS, SB_HEAD_DIM))
        else:
            w = ret_w_qkvg[layer]
            gn = ret_gn_g[layer]
            (q, k), (q_s, k_s), _ = project(hp, hs, w, [(0, BF16, RET_QK_DIM ** -0.5), (d, BF16, 1.0)], d, COL_TILE,
                                            rope=rope_p, rope_s=rope_s)
            (v, gate), (v_s, gate_s), w_o = project(hp, hs, w, [(2 * d, BF16, None), (4 * d, F32, None)], 2 * d,
                                                    COL_TILE, w_copy_src=ret_w_o[layer])
            mix_p, st_p = retention_prompt(q, k, v, gate, gn, log_gamma, batch, seq)
            rsp.append(st_p)
            mix_s, st_s = retention_decode(q_s[:ms], k_s[:ms], v_s[:ms], gate_s[:ms], gn, log_gamma,
                                           state_ret[layer])
            rss.append(st_s)

        xp, xs, hp, hs = down_residual(mix_p, pad_rows(mix_s), w_o[None, None], (0, 0), xp, xs, g[3], g[4], 1.0)

        g_next = norm_g[i + 1, 0] if i + 1 < depth else None
        xp, xs, hp, hs = ffn(xp, xs, hp, hs, 1, g[5], g_next)

    return (xp.reshape(batch, seq, d), xs[:ms].reshape(dec_batch, dec_seq, d), _stack(kp), _stack(vp),
            _stack(ksm), _stack(vsm), _stack(rsp), _stack(rss))
```

```python
import functools

import jax
import jax.numpy as jnp
from jax import lax
from jax.experimental import pallas as pl
from jax.experimental.pallas import tpu as pltpu

F32 = jnp.float32
BF16 = jnp.bfloat16

EPS = 1e-6
SB_HEADS = 16
SB_HEAD_DIM = 128
PAGE_SIZE = 128
RET_HEADS = 8
RET_QK_DIM = 256
RET_V_DIM = 512
RET_CHUNK = 128
ROPE_BASE = 10000.0

VMEM_LIMIT_BYTES = 60 * 1024 * 1024
LANES = 128

ROW_TILE = 1024
DOWN_ROW_TILE = 1024
DOWN_ROW_SPLITS = 2
COL_TILE = 512
QKV_COL_TILE = 256
K_TILE = 512
ATT_BLOCK = 256
DECODE_PAGES_PER_STEP = 4


def _params(sem):
    return pltpu.CompilerParams(dimension_semantics=sem, vmem_limit_bytes=VMEM_LIMIT_BYTES)


def _rms(x, g):
    return x * lax.rsqrt(jnp.mean(x * x, axis=-1, keepdims=True) + EPS) * g


def _row_tile(m, want):
    return want if m % want == 0 else m


def _prenorm_kernel(x_ref, g_ref, o_ref):
    o_ref[...] = _rms(x_ref[...], g_ref[...]).astype(o_ref.dtype)


def prenorm(x, g):
    m, d = x.shape
    tm = _row_tile(m, DOWN_ROW_TILE)
    return pl.pallas_call(
        _prenorm_kernel,
        grid=(m // tm,),
        in_specs=[pl.BlockSpec((tm, d), lambda i: (i, 0)), pl.BlockSpec((1, d), lambda i: (0, 0))],
        out_specs=pl.BlockSpec((tm, d), lambda i: (i, 0)),
        out_shape=jax.ShapeDtypeStruct((m, d), BF16),
        compiler_params=_params(("arbitrary",)),
        name="prenorm",
    )(x, g.reshape(1, d))


def _swiglu_kernel(*refs, has_copy):
    if has_copy:
        a_ref, as_ref, wg_ref, wu_ref, wsrc_ref, o_ref, os_ref, wcopy_ref, wg_bf, wu_bf = refs
    else:
        a_ref, as_ref, wg_ref, wu_ref, o_ref, os_ref, wg_bf, wu_bf = refs
    first = pl.program_id(1) == 0
    tm = a_ref.shape[0]

    def act(a):
        gate = jnp.dot(a, wg_bf[...], preferred_element_type=F32)
        up = jnp.dot(a, wu_bf[...], preferred_element_type=F32)
        return (gate * jax.nn.sigmoid(gate) * up).astype(o_ref.dtype)

    @pl.when(first)
    def _():
        wg_bf[...] = wg_ref[...].astype(BF16)
        wu_bf[...] = wu_ref[...].astype(BF16)
        y = act(jnp.concatenate([a_ref[...], as_ref[...]], axis=0))
        o_ref[...] = y[:tm]
        os_ref[...] = y[tm:]

    @pl.when(jnp.logical_not(first))
    def _():
        o_ref[...] = act(a_ref[...])

    if has_copy:
        wcopy_ref[...] = wsrc_ref[...].astype(BF16)


def swiglu_up(h, hs, w_gate, w_up, widx, w_copy_src=None):
    m, k = h.shape
    n = w_gate.shape[-1]
    tm, tn = _row_tile(m, ROW_TILE), COL_TILE
    i0, i1 = widx
    wspec = pl.BlockSpec((None, None, k, tn), lambda j, i: (i0, i1, 0, j))
    in_specs = [pl.BlockSpec((tm, k), lambda j, i: (i, 0)), pl.BlockSpec(hs.shape, lambda j, i: (0, 0)), wspec, wspec]
    out_specs = [pl.BlockSpec((tm, tn), lambda j, i: (i, j)), pl.BlockSpec((hs.shape[0], tn), lambda j, i: (0, j))]
    out_shape = [jax.ShapeDtypeStruct((m, n), BF16), jax.ShapeDtypeStruct((hs.shape[0], n), BF16)]
    args = [h, hs, w_gate, w_up]
    if w_copy_src is not None:
        r, d = w_copy_src.shape[-2:]
        n_row = m // tm
        slab = r // ((n // tn) * n_row)
        in_specs.append(pl.BlockSpec((None, None, slab, d), lambda j, i: (i0, i1, j * n_row + i, 0)))
        out_specs.append(pl.BlockSpec((slab, d), lambda j, i: (j * n_row + i, 0)))
        out_shape.append(jax.ShapeDtypeStruct((r, d), BF16))
        args.append(w_copy_src)
    return pl.pallas_call(
        functools.partial(_swiglu_kernel, has_copy=w_copy_src is not None),
        grid=(n // tn, m // tm),
        in_specs=in_specs,
        out_specs=out_specs,
        out_shape=out_shape,
        scratch_shapes=[pltpu.VMEM((k, tn), BF16), pltpu.VMEM((k, tn), BF16)],
        compiler_params=_params(("arbitrary", "arbitrary")),
        name="swiglu_up",
    )(*args)


def _proj_kernel(*refs, n_out, ropes, has_copy):
    has_rope = any(r is not None for r in ropes)
    refs = list(refs)
    a_ref, as_ref = refs[:2]
    w_refs = refs[2:2 + n_out]
    pos = 2 + n_out
    if has_rope:
        cos_ref, sin_ref, cos_s_ref, sin_s_ref = refs[pos:pos + 4]
        pos += 4
    if has_copy:
        wsrc_ref = refs[pos]
        pos += 1
    o_refs = refs[pos:pos + n_out]
    os_refs = refs[pos + n_out:pos + 2 * n_out]
    pos += 2 * n_out
    if has_copy:
        wcopy_ref = refs[pos]
        pos += 1
    w_bfs = refs[pos:]
    first = pl.program_id(1) == 0
    tm = a_ref.shape[0]

    def outputs(a, cos, sin):
        ys = []
        for w_bf, rope_scale in zip(w_bfs, ropes):
            y = jnp.dot(a, w_bf[...], preferred_element_type=F32)
            if rope_scale is not None:
                half = cos.shape[-1]
                parts = []
                for hd in range(y.shape[-1] // (2 * half)):
                    x1 = y[:, hd * 2 * half:hd * 2 * half + half]
                    x2 = y[:, hd * 2 * half + half:(hd + 1) * 2 * half]
                    parts += [(x1 * cos - x2 * sin) * rope_scale, (x1 * sin + x2 * cos) * rope_scale]
                y = jnp.concatenate(parts, axis=1)
            ys.append(y)
        return ys

    @pl.when(first)
    def _():
        for w_ref, w_bf in zip(w_refs, w_bfs):
            w_bf[...] = w_ref[...].astype(BF16)
        cos = jnp.concatenate([cos_ref[...], cos_s_ref[...]], axis=0) if has_rope else None
        sin = jnp.concatenate([sin_ref[...], sin_s_ref[...]], axis=0) if has_rope else None
        ys = outputs(jnp.concatenate([a_ref[...], as_ref[...]], axis=0), cos, sin)
        for y, o_ref, os_ref in zip(ys, o_refs, os_refs):
            o_ref[...] = y[:tm].astype(o_ref.dtype)
            os_ref[...] = y[tm:].astype(os_ref.dtype)

    @pl.when(jnp.logical_not(first))
    def _():
        ys = outputs(a_ref[...], cos_ref[...] if has_rope else None, sin_ref[...] if has_rope else None)
        for y, o_ref in zip(ys, o_refs):
            o_ref[...] = y.astype(o_ref.dtype)

    if has_copy:
        wcopy_ref[...] = wsrc_ref[...].astype(BF16)


def project(h, hs, w, outs, ncols, tn, rope=None, rope_s=None, w_copy_src=None):
    m, k = h.shape
    ms = hs.shape[0]
    tm = _row_tile(m, ROW_TILE)
    ncol_steps = ncols // tn
    ropes = tuple(o[2] for o in outs)
    in_specs = [pl.BlockSpec((tm, k), lambda j, i: (i, 0)), pl.BlockSpec((ms, k), lambda j, i: (0, 0))]
    for c0, _, _ in outs:
        in_specs.append(pl.BlockSpec((k, tn), functools.partial(lambda j, i, off: (0, off + j), off=c0 // tn)))
    args = [h, hs] + [w] * len(outs)
    if rope is not None:
        half = rope[0].shape[-1]
        in_specs += ([pl.BlockSpec((tm, half), lambda j, i: (i, 0))] * 2
                     + [pl.BlockSpec((ms, half), lambda j, i: (0, 0))] * 2)
        args += list(rope) + list(rope_s)
    out_specs = ([pl.BlockSpec((tm, tn), lambda j, i: (i, j))] * len(outs)
                 + [pl.BlockSpec((ms, tn), lambda j, i: (0, j))] * len(outs))
    out_shape = ([jax.ShapeDtypeStruct((m, ncols), o[1]) for o in outs]
                 + [jax.ShapeDtypeStruct((ms, ncols), F32) for _ in outs])
    if w_copy_src is not None:
        r, d = w_copy_src.shape
        n_row = m // tm
        slab = r // (ncol_steps * n_row)
        in_specs.append(pl.BlockSpec((slab, d), lambda j, i: (j * n_row + i, 0)))
        args.append(w_copy_src)
        out_specs.append(pl.BlockSpec((slab, d), lambda j, i: (j * n_row + i, 0)))
        out_shape.append(jax.ShapeDtypeStruct((r, d), BF16))
    res = pl.pallas_call(
        functools.partial(_proj_kernel, n_out=len(outs), ropes=ropes, has_copy=w_copy_src is not None),
        grid=(ncol_steps, m // tm),
        in_specs=in_specs,
        out_specs=out_specs,
        out_shape=out_shape,
        scratch_shapes=[pltpu.VMEM((k, tn), BF16)] * len(outs),
        compiler_params=_params(("arbitrary", "arbitrary")),
        name="project_rope" if rope is not None else "project",
    )(*args)
    n = len(outs)
    return res[:n], res[n:2 * n], (res[2 * n] if w_copy_src is not None else None)


def _down_kernel(*refs, res_scale, nk, n_tiles, n_chunks, has_next, row_splits):
    if has_next:
        (a_ref, as_ref, w_ref, x_ref, xs_ref, gpost_ref, gnext_ref,
         xo_ref, xso_ref, ho_ref, hso_ref, acc, acc_s) = refs
    else:
        a_ref, as_ref, w_ref, x_ref, xs_ref, gpost_ref, xo_ref, xso_ref, acc, acc_s = refs
        gnext_ref = ho_ref = hso_ref = None
    i = pl.program_id(0)
    kk = pl.program_id(1)
    tm = acc.shape[1]
    rows = tm // row_splits
    chunk = tm // n_chunks
    slot = lax.rem(i, 2)
    has_matmul = i < n_tiles

    @pl.when(jnp.logical_and(i == 0, kk == 0))
    def _():
        acc[...] = jnp.zeros_like(acc)
        acc_s[...] = jnp.zeros_like(acc_s)

    @pl.when(jnp.logical_and(jnp.logical_and(i > 0, has_matmul), kk == 0))
    def _():
        acc[slot] = jnp.zeros(acc.shape[1:], acc.dtype)

    def accumulate(with_sample_rows):
        w = w_ref[...]
        for r in range(row_splits):
            sl = slice(r * rows, (r + 1) * rows)
            a = a_ref[sl, :].astype(BF16)
            if with_sample_rows and r == row_splits - 1:
                d = jnp.dot(jnp.concatenate([a, as_ref[...].astype(BF16)], axis=0), w, preferred_element_type=F32)
                acc[slot, sl, :] += d[:rows]
                acc_s[...] += d[rows:]
            else:
                acc[slot, sl, :] += jnp.dot(a, w, preferred_element_type=F32)

    def finish(x, total, o_ref, h_ref):
        xn = x + res_scale * _rms(total, gpost_ref[...])
        o_ref[...] = xn
        if has_next:
            h_ref[...] = _rms(xn, gnext_ref[...]).astype(h_ref.dtype)

    def finish_chunk():
        r0 = pl.multiple_of(jnp.minimum(kk, n_chunks - 1) * chunk, chunk)
        finish(x_ref[...], acc[1 - slot, pl.ds(r0, chunk), :], xo_ref, ho_ref)

    @pl.when(i == 0)
    def _():
        finish_chunk()
        accumulate(True)

    @pl.when(jnp.logical_and(i > 0, has_matmul))
    def _():
        finish_chunk()
        accumulate(False)

    @pl.when(jnp.logical_not(has_matmul))
    def _():
        finish_chunk()

    @pl.when(jnp.logical_and(i == 1, kk == 0))
    def _():
        finish(xs_ref[...], acc_s[...], xso_ref, hso_ref)


def down_residual(a, a_s, w, widx, x, x_s, g_post, g_next, res_scale):
    m, k = a.shape
    ms = a_s.shape[0]
    d = w.shape[-1]
    tm, tk = _row_tile(m, DOWN_ROW_TILE), K_TILE
    row_splits = DOWN_ROW_SPLITS if tm % (16 * DOWN_ROW_SPLITS) == 0 else 1
    nk = k // tk
    n_tiles = m // tm
    n_chunks = max(c for c in (1, 2, 4, 8) if c <= nk and tm % (16 * c) == 0)
    chunk = tm // n_chunks
    i0, i1 = widx
    has_next = g_next is not None

    def chunk_map(i, kk):
        return (jnp.where(i == 0, 0, (i - 1) * n_chunks + jnp.minimum(kk, n_chunks - 1)), 0)

    crow = pl.BlockSpec((chunk, d), chunk_map)
    srow = pl.BlockSpec((ms, d), lambda i, kk: (0, 0))
    vec = pl.BlockSpec((1, d), lambda i, kk: (0, 0))
    in_specs = [pl.BlockSpec((tm, tk), lambda i, kk: (jnp.minimum(i, n_tiles - 1), kk)),
                pl.BlockSpec((ms, tk), lambda i, kk: (0, kk)),
                pl.BlockSpec((None, None, tk, d), lambda i, kk: (i0, i1, kk, 0)), crow, srow, vec]
    args = [a, a_s, w, x, x_s, g_post.reshape(1, d)]
    out_specs = [crow, srow]
    out_shape = [jax.ShapeDtypeStruct((m, d), F32), jax.ShapeDtypeStruct((ms, d), F32)]
    if has_next:
        in_specs.append(vec)
        args.append(g_next.reshape(1, d))
        out_specs += [crow, srow]
        out_shape += [jax.ShapeDtypeStruct((m, d), BF16), jax.ShapeDtypeStruct((ms, d), BF16)]
    outs = pl.pallas_call(
        functools.partial(_down_kernel, res_scale=res_scale, nk=nk, n_tiles=n_tiles, n_chunks=n_chunks,
                          has_next=has_next, row_splits=row_splits),
        grid=(n_tiles + 1, nk),
        in_specs=in_specs,
        out_specs=out_specs,
        out_shape=out_shape,
        scratch_shapes=[pltpu.VMEM((2, tm, d), F32), pltpu.VMEM((ms, d), F32)],
        compiler_params=_params(("arbitrary", "arbitrary")),
        name="down_residual",
    )(*args)
    return tuple(outs) if has_next else (outs[0], outs[1], None, None)


def _softplus(z):
    return jnp.maximum(z, 0.0) + jnp.log(1.0 + jnp.exp(-jnp.abs(z)))


def _suffix_sum(sp, tri):
    hi = sp.astype(BF16)
    lo = (sp - hi.astype(F32)).astype(BF16)
    return jnp.dot(hi, tri, preferred_element_type=F32) + jnp.dot(lo, tri, preferred_element_type=F32)


def _sb_prompt_kernel(bias_ref, q_ref, k_ref, v_ref, o_ref, acc, run, *, blk, scale):
    nblk = q_ref.shape[0] // blk
    bias = bias_ref[pl.program_id(1)]
    row = lax.broadcasted_iota(jnp.int32, (blk, blk), 0)
    col = lax.broadcasted_iota(jnp.int32, (blk, blk), 1)
    tri = (row >= col).astype(BF16)
    visible = col < row
    k_bf = k_ref[...].astype(BF16)
    v_bf = v_ref[...].astype(BF16)

    def mask_diagonal(x):
        top = jnp.where(visible, x[:blk], 0.0)
        return top if x.shape[0] == blk else jnp.concatenate([top, x[blk:]], axis=0)

    acc[...] = jnp.zeros_like(acc)
    run[...] = jnp.zeros_like(run)

    for j in reversed(range(nblk)):
        r0 = j * blk
        z = lax.dot_general(q_ref[r0:, :], k_bf[r0:r0 + blk], (((1,), (1,)), ((), ())),
                            preferred_element_type=F32) * scale + bias
        local = _suffix_sum(mask_diagonal(_softplus(z)), tri)
        p = mask_diagonal(jnp.exp(z - local))
        pv = jnp.dot(p.astype(BF16), v_bf[r0:r0 + blk], preferred_element_type=F32)
        acc[r0:, :] += jnp.exp(-run[r0:, :]) * pv
        run[r0:, :] += local[:, 0:1]

    o_ref[...] = acc[...].astype(o_ref.dtype)


def sb_attention_prompt(q, k, v, bias, batch, seq):
    dh = SB_HEAD_DIM
    spec = pl.BlockSpec((seq, dh), lambda b, h: (b, h))
    return pl.pallas_call(
        functools.partial(_sb_prompt_kernel, blk=ATT_BLOCK, scale=dh ** -0.5),
        grid=(batch, SB_HEADS),
        in_specs=[pl.BlockSpec(memory_space=pltpu.SMEM), spec, spec, spec],
        out_specs=spec,
        out_shape=jax.ShapeDtypeStruct(q.shape, BF16),
        scratch_shapes=[pltpu.VMEM((seq, dh), F32), pltpu.VMEM((seq, 1), F32)],
        compiler_params=_params(("arbitrary", "arbitrary")),
        name="sb_attention_prompt",
    )(bias, q, k, v)


def _sb_decode_kernel(pt_ref, q_ref, bias_ref, *refs, scale, n_steps, group):
    del pt_ref
    k_refs, v_refs = refs[:group], refs[group:2 * group]
    o_ref, acc, run, z_rows, a_flat, g_rows = refs[2 * group:]
    p = pl.program_id(1)
    page, heads, dh = k_refs[0].shape
    flat = page * heads
    n_groups = flat // LANES

    @pl.when(p == 0)
    def _():
        acc[...] = jnp.zeros_like(acc)
        run[...] = jnp.zeros_like(run)

    lane = lax.broadcasted_iota(jnp.int32, (heads, flat), 1)
    own = lax.rem(lane, heads) == lax.broadcasted_iota(jnp.int32, (heads, flat), 0)
    src = lax.broadcasted_iota(jnp.int32, (LANES, 2 * LANES), 0)
    dst = lax.broadcasted_iota(jnp.int32, (LANES, 2 * LANES), 1)
    same_head = lax.rem(src, heads) == lax.rem(dst, heads)
    newer = (src // heads) >= (lax.rem(dst, LANES) // heads)
    sel = (same_head & (newer | (dst >= LANES))).astype(BF16)
    q = q_ref[...].astype(BF16)
    bias = bias_ref[...]

    def logits(g):
        kf = k_refs[g][...].reshape(flat, dh).astype(BF16)
        zt = lax.dot_general(q, kf, (((1,), (1,)), ((), ())), preferred_element_type=F32)
        zf = jnp.sum(jnp.where(own, zt, 0.0), axis=0, keepdims=True)
        for c in range(n_groups):
            z_rows[g, c:c + 1, :] = zf[:, c * LANES:(c + 1) * LANES]
        z = z_rows[g] * scale + bias
        both = _suffix_sum(_softplus(z), sel)
        local, group_total = both[:, :LANES], both[:, LANES:]
        newer_groups = jnp.zeros((1, LANES), F32)
        for c in range(n_groups - 1, -1, -1):
            g_rows[g, c:c + 1, :] = newer_groups
            newer_groups = newer_groups + group_total[c:c + 1, :]
        return z - (local + g_rows[g]), newer_groups

    pages = [logits(g) for g in range(group)]
    seen = run[...]
    for g, (log_a, page_total) in enumerate(pages):
        a = jnp.exp(log_a - seen)
        seen = seen + page_total
        for c in range(n_groups):
            a_flat[g, :, c * LANES:(c + 1) * LANES] = a[c:c + 1, :]
        a_own = jnp.where(own, jnp.broadcast_to(a_flat[g], (heads, flat)), 0.0).astype(BF16)
        vf = v_refs[g][...].reshape(flat, dh).astype(BF16)
        acc[...] += jnp.dot(a_own, vf, preferred_element_type=F32)
    run[...] = seen

    @pl.when(p == n_steps - 1)
    def _():
        o_ref[...] = acc[...]


def sb_attention_decode(q, k_pool, v_pool, layer, page_table, bias):
    b, heads, dh = q.shape
    n_pages = page_table.shape[1]
    page = k_pool.shape[2]
    n_groups = page * heads // LANES
    group = max(g for g in range(1, DECODE_PAGES_PER_STEP + 1) if n_pages % g == 0)
    n_steps = n_pages // group

    def pool_spec(g):
        def page_map(bi, p, pt):
            return (layer, pt[bi * n_pages + (n_pages - 1 - (p * group + g))], 0, 0, 0)
        return pl.BlockSpec((None, None, page, heads, dh), page_map)

    pool_specs = [pool_spec(g) for g in range(group)]
    q_spec = pl.BlockSpec((None, heads, dh), lambda bi, p, pt: (bi, 0, 0))
    bias_lanes = jnp.tile(bias, LANES // heads).reshape(1, LANES)
    return pl.pallas_call(
        functools.partial(_sb_decode_kernel, scale=dh ** -0.5, n_steps=n_steps, group=group),
        grid_spec=pltpu.PrefetchScalarGridSpec(
            num_scalar_prefetch=1,
            grid=(b, n_steps),
            in_specs=[q_spec, pl.BlockSpec((1, LANES), lambda bi, p, pt: (0, 0))] + pool_specs + pool_specs,
            out_specs=q_spec,
            scratch_shapes=[pltpu.VMEM((heads, dh), F32), pltpu.VMEM((1, LANES), F32),
                            pltpu.VMEM((group, n_groups, LANES), F32), pltpu.VMEM((group, 1, page * heads), F32),
                            pltpu.VMEM((group, n_groups, LANES), F32)],
        ),
        out_shape=jax.ShapeDtypeStruct((b, heads, dh), F32),
        compiler_params=_params(("arbitrary", "arbitrary")),
        name="sb_attention_decode",
    )(page_table.reshape(-1), q, bias_lanes, *([k_pool] * group), *([v_pool] * group))


def _group_norm_gate(o, gate, gn):
    mu = jnp.mean(o, axis=-1, keepdims=True)
    var = jnp.mean((o - mu) ** 2, axis=-1, keepdims=True)
    return (o - mu) * lax.rsqrt(var + EPS) * gn * (gate * jax.nn.sigmoid(gate))


def _ret_prompt_kernel(lg_ref, q_ref, k_ref, v_ref, g_ref, gn_ref, o_ref, s_ref, state, *, chunk, n_chunks):
    lg = lg_ref[pl.program_id(1)]
    n = lax.broadcasted_iota(jnp.int32, (chunk, chunk), 0)
    mcol = lax.broadcasted_iota(jnp.int32, (chunk, chunk), 1)
    diff = (n - mcol).astype(F32)
    decay = jnp.where(diff >= 0, jnp.exp(lg * jnp.maximum(diff, 0.0)), 0.0)
    idx = lax.broadcasted_iota(jnp.int32, (chunk, 1), 0).astype(F32)
    xi = jnp.exp(lg * (idx + 1.0))
    zeta = jnp.exp(lg * (chunk - 1.0 - idx))
    chunk_decay = jnp.exp(lg * jnp.full((1, state.shape[1]), float(chunk), F32))
    gn = gn_ref[...]

    state[...] = jnp.zeros_like(state)

    for c in range(n_chunks):
        rows = slice(c * chunk, (c + 1) * chunk)
        qc, kc, vc = q_ref[rows, :], k_ref[rows, :], v_ref[rows, :]
        s0 = state[...]
        scores = lax.dot_general(qc, kc, (((1,), (1,)), ((), ())), preferred_element_type=F32) * decay
        inner = jnp.dot(scores.astype(BF16), vc, preferred_element_type=F32)
        cross = jnp.dot(qc, s0.astype(BF16), preferred_element_type=F32) * xi
        kz = (kc.astype(F32) * zeta).astype(BF16)
        state[...] = chunk_decay * s0 + lax.dot_general(kz, vc, (((0,), (0,)), ((), ())),
                                                        preferred_element_type=F32)
        o_ref[rows, :] = _group_norm_gate(inner + cross, g_ref[rows, :], gn).astype(o_ref.dtype)
    s_ref[...] = state[...]


def retention_prompt(q, k, v, g, gn, log_gamma, batch, seq):
    dk, dv = RET_QK_DIM, RET_V_DIM
    qk_spec = pl.BlockSpec((seq, dk), lambda b, h: (b, h))
    v_spec = pl.BlockSpec((seq, dv), lambda b, h: (b, h))
    return pl.pallas_call(
        functools.partial(_ret_prompt_kernel, chunk=RET_CHUNK, n_chunks=seq // RET_CHUNK),
        grid=(batch, RET_HEADS),
        in_specs=[pl.BlockSpec(memory_space=pltpu.SMEM), qk_spec, qk_spec, v_spec, v_spec,
                  pl.BlockSpec((1, dv), lambda b, h: (0, h))],
        out_specs=[v_spec, pl.BlockSpec((None, None, dk, dv), lambda b, h: (b, h, 0, 0))],
        out_shape=[jax.ShapeDtypeStruct(v.shape, BF16),
                   jax.ShapeDtypeStruct((batch, RET_HEADS, dk, dv), F32)],
        scratch_shapes=[pltpu.VMEM((dk, dv), F32)],
        compiler_params=_params(("arbitrary", "arbitrary")),
        name="retention_prompt",
    )(log_gamma, q, k, v, g, gn.reshape(1, -1))


def _ret_decode_kernel(lg_ref, q_ref, k_ref, v_ref, g_ref, gn_ref, s0_ref, o_ref, s_ref):
    lg = lg_ref[pl.program_id(1)]
    dv = v_ref.shape[-1]
    gamma = jnp.exp(lg * jnp.ones((1, dv), F32))
    q, k, v, s0 = q_ref[...], k_ref[...], v_ref[...], s0_ref[...]
    qk = jnp.sum(q * k, axis=0, keepdims=True)
    cross = jnp.sum(q * s0, axis=0, keepdims=True) * gamma
    o = qk * v + cross
    s_ref[...] = gamma * s0 + k * v
    o_ref[...] = _group_norm_gate(o, g_ref[...], gn_ref[...]).astype(o_ref.dtype)


def retention_decode(q, k, v, g, gn, log_gamma, state):
    b = q.shape[0]
    h, dk, dv = RET_HEADS, RET_QK_DIM, RET_V_DIM
    col_spec = pl.BlockSpec((None, None, dk, 1), lambda bi, hi: (bi, hi, 0, 0))
    row_spec = pl.BlockSpec((None, None, 1, dv), lambda bi, hi: (bi, hi, 0, 0))
    st_spec = pl.BlockSpec((None, None, dk, dv), lambda bi, hi: (bi, hi, 0, 0))
    o, s_new = pl.pallas_call(
        _ret_decode_kernel,
        grid=(b, h),
        in_specs=[pl.BlockSpec(memory_space=pltpu.SMEM), col_spec, col_spec, row_spec, row_spec,
                  pl.BlockSpec((None, 1, dv), lambda bi, hi: (hi, 0, 0)), st_spec],
        out_specs=[row_spec, st_spec],
        out_shape=[jax.ShapeDtypeStruct((b, h, 1, dv), BF16), jax.ShapeDtypeStruct(state.shape, F32)],
        compiler_params=_params(("arbitrary", "arbitrary")),
        name="retention_decode",
    )(log_gamma, q.reshape(b, h, dk, 1), k.reshape(b, h, dk, 1), v.reshape(b, h, 1, dv),
      g.reshape(b, h, 1, dv), gn.reshape(h, 1, dv), state)
    return o.reshape(b, h * dv), s_new


def _stack(xs):
    return xs[0][None] if len(xs) == 1 else jnp.stack(xs)


def _rope_tables(pos, rows):
    half = RET_QK_DIM // 2
    inv = ROPE_BASE ** (-jnp.arange(half, dtype=F32) / half)
    ang = pos.astype(F32)[:, None] * inv[None, :]
    cos, sin = jnp.cos(ang), jnp.sin(ang)
    reps = rows // pos.shape[0]
    return jnp.tile(cos, (reps, 1)), jnp.tile(sin, (reps, 1))


SAMPLE_ROWS = 16


def kernel(x_prompt, x_sample, cache_k, cache_v, state_ret, page_table, norm_g, ffn_w_gate, ffn_w_up, ffn_w_down,
           sb_w_qkv, sb_w_o, sb_bias, ret_w_qkvg, ret_gn_g, ret_w_o):
    batch, seq, d = x_prompt.shape
    dec_batch, dec_seq, _ = x_sample.shape
    depth = norm_g.shape[0]
    past_len = page_table.shape[1] * PAGE_SIZE
    assert dec_seq == 1, "the decode kernels take one new token per sequence"
    mp, ms = batch * seq, dec_batch * dec_seq
    assert ms <= SAMPLE_ROWS

    def pad_rows(t):
        return jnp.concatenate([t, jnp.zeros((SAMPLE_ROWS - ms, t.shape[1]), t.dtype)], axis=0)

    log_gamma = jnp.log1p(-jnp.exp2(-5.0 - jnp.arange(RET_HEADS, dtype=F32)))
    rope_p = _rope_tables(jnp.arange(seq, dtype=jnp.int32), mp)
    rope_s = _rope_tables(past_len + jnp.arange(dec_seq, dtype=jnp.int32), ms)
    rope_s = (pad_rows(rope_s[0]), pad_rows(rope_s[1]))

    xp, xs = x_prompt.reshape(mp, d), pad_rows(x_sample.reshape(ms, d))
    hp, hs = prenorm(xp, norm_g[0, 0]), prenorm(xs, norm_g[0, 0])
    kp, vp, ksm, vsm, rsp, rss = [], [], [], [], [], []

    for i in range(depth):
        g = norm_g[i]
        layer = i // 2

        def ffn(x, x_s, h, h_s, j, g_post, g_next):
            act, act_s, w_down = swiglu_up(h, h_s, ffn_w_gate, ffn_w_up, (i, j), w_copy_src=ffn_w_down)
            return down_residual(act, act_s, w_down[None, None], (0, 0), x, x_s, g_post, g_next, 0.5)

        xp, xs, hp, hs = ffn(xp, xs, hp, hs, 0, g[1], g[2])

        if i % 2 == 0:
            w = sb_w_qkv[layer]
            bias = sb_bias[layer]
            (q, k, v), (q_s, k_s, v_s), w_o = project(
                hp, hs, w, [(0, BF16, None), (d, F32, None), (2 * d, F32, None)], d, QKV_COL_TILE,
                w_copy_src=sb_w_o[layer])
            mix_p = sb_attention_prompt(q, k, v, bias, batch, seq)
            kp.append(k.reshape(batch, seq, SB_HEADS, SB_HEAD_DIM))
            vp.append(v.reshape(batch, seq, SB_HEADS, SB_HEAD_DIM))
            mix_s = sb_attention_decode(q_s[:ms].reshape(ms, SB_HEADS, SB_HEAD_DIM), cache_k, cache_v, layer,
                                        page_table, bias).reshape(ms, d)
            ksm.append(k_s[:ms].reshape(dec_batch, dec_seq, SB_HEADS, SB_HEAD_DIM))
            vsm.append(v_s[:ms].reshape(dec_batch, dec_seq, SB_HEADS, SB_HEAD_DIM))
        else:
            w = ret_w_qkvg[layer]
            gn = ret_gn_g[layer]
            (q, k), (q_s, k_s), _ = project(hp, hs, w, [(0, BF16, RET_QK_DIM ** -0.5), (d, BF16, 1.0)], d, COL_TILE,
                                            rope=rope_p, rope_s=rope_s)
            (v, gate), (v_s, gate_s), w_o = project(hp, hs, w, [(2 * d, BF16, None), (4 * d, F32, None)], 2 * d,
                                                    COL_TILE, w_copy_src=ret_w_o[layer])
            mix_p, st_p = retention_prompt(q, k, v, gate, gn, log_gamma, batch, seq)
            rsp.append(st_p)
            mix_s, st_s = retention_decode(q_s[:ms], k_s[:ms], v_s[:ms], gate_s[:ms], gn, log_gamma,
                                           state_ret[layer])
            rss.append(st_s)

        xp, xs, hp, hs = down_residual(mix_p, pad_rows(mix_s), w_o[None, None], (0, 0), xp, xs, g[3], g[4], 1.0)

        g_next = norm_g[i + 1, 0] if i + 1 < depth else None
        xp, xs, hp, hs = ffn(xp, xs, hp, hs, 1, g[5], g_next)

    return (xp.reshape(batch, seq, d), xs[:ms].reshape(dec_batch, dec_seq, d), _stack(kp), _stack(vp),
            _stack(ksm), _stack(vsm), _stack(rsp), _stack(rss))
```

```python
import functools

import jax
import jax.numpy as jnp
from jax import lax
from jax.experimental import pallas as pl
from jax.experimental.pallas import tpu as pltpu

F32 = jnp.float32
BF16 = jnp.bfloat16

EPS = 1e-6
SB_HEADS = 16
SB_HEAD_DIM = 128
PAGE_SIZE = 128
RET_HEADS = 8
RET_QK_DIM = 256
RET_V_DIM = 512
RET_CHUNK = 128
ROPE_BASE = 10000.0

VMEM_LIMIT_BYTES = 60 * 1024 * 1024
LANES = 128

ROW_TILE = 1024
DOWN_ROW_TILE = 1024
DOWN_ROW_SPLITS = 2
COL_TILE = 512
QKV_COL_TILE = 256
K_TILE = 512
ATT_BLOCK = 256
DECODE_PAGES_PER_STEP = 4


def _params(sem):
    return pltpu.CompilerParams(dimension_semantics=sem, vmem_limit_bytes=VMEM_LIMIT_BYTES)


def _rms(x, g):
    return x * lax.rsqrt(jnp.mean(x * x, axis=-1, keepdims=True) + EPS) * g


def _row_tile(m, want):
    return want if m % want == 0 else m


def _prenorm_kernel(x_ref, g_ref, o_ref):
    o_ref[...] = _rms(x_ref[...], g_ref[...]).astype(o_ref.dtype)


def prenorm(x, g):
    m, d = x.shape
    tm = _row_tile(m, DOWN_ROW_TILE)
    return pl.pallas_call(
        _prenorm_kernel,
        grid=(m // tm,),
        in_specs=[pl.BlockSpec((tm, d), lambda i: (i, 0)), pl.BlockSpec((1, d), lambda i: (0, 0))],
        out_specs=pl.BlockSpec((tm, d), lambda i: (i, 0)),
        out_shape=jax.ShapeDtypeStruct((m, d), BF16),
        compiler_params=_params(("arbitrary",)),
        name="prenorm",
    )(x, g.reshape(1, d))


def _swiglu_kernel(*refs, has_copy):
    if has_copy:
        a_ref, as_ref, wg_ref, wu_ref, wsrc_ref, o_ref, os_ref, wcopy_ref, wg_bf, wu_bf = refs
    else:
        a_ref, as_ref, wg_ref, wu_ref, o_ref, os_ref, wg_bf, wu_bf = refs
    first = pl.program_id(1) == 0
    tm = a_ref.shape[0]

    def act(a):
        gate = jnp.dot(a, wg_bf[...], preferred_element_type=F32)
        up = jnp.dot(a, wu_bf[...], preferred_element_type=F32)
        return (gate * jax.nn.sigmoid(gate) * up).astype(o_ref.dtype)

    @pl.when(first)
    def _():
        wg_bf[...] = wg_ref[...].astype(BF16)
        wu_bf[...] = wu_ref[...].astype(BF16)
        y = act(jnp.concatenate([a_ref[...], as_ref[...]], axis=0))
        o_ref[...] = y[:tm]
        os_ref[...] = y[tm:]

    @pl.when(jnp.logical_not(first))
    def _():
        o_ref[...] = act(a_ref[...])

    if has_copy:
        wcopy_ref[...] = wsrc_ref[...].astype(BF16)


def swiglu_up(h, hs, w_gate, w_up, widx, w_copy_src=None):
    m, k = h.shape
    n = w_gate.shape[-1]
    tm, tn = _row_tile(m, ROW_TILE), COL_TILE
    i0, i1 = widx
    wspec = pl.BlockSpec((None, None, k, tn), lambda j, i: (i0, i1, 0, j))
    in_specs = [pl.BlockSpec((tm, k), lambda j, i: (i, 0)), pl.BlockSpec(hs.shape, lambda j, i: (0, 0)), wspec, wspec]
    out_specs = [pl.BlockSpec((None, tm, tn), lambda j, i: (j, i, 0)),
                 pl.BlockSpec((None, hs.shape[0], tn), lambda j, i: (j, 0, 0))]
    out_shape = [jax.ShapeDtypeStruct((n // tn, m, tn), BF16), jax.ShapeDtypeStruct((n // tn, hs.shape[0], tn), BF16)]
    args = [h, hs, w_gate, w_up]
    if w_copy_src is not None:
        r, d = w_copy_src.shape[-2:]
        n_row = m // tm
        slab = r // ((n // tn) * n_row)
        in_specs.append(pl.BlockSpec((None, None, slab, d), lambda j, i: (i0, i1, j * n_row + i, 0)))
        out_specs.append(pl.BlockSpec((slab, d), lambda j, i: (j * n_row + i, 0)))
        out_shape.append(jax.ShapeDtypeStruct((r, d), BF16))
        args.append(w_copy_src)
    return pl.pallas_call(
        functools.partial(_swiglu_kernel, has_copy=w_copy_src is not None),
        grid=(n // tn, m // tm),
        in_specs=in_specs,
        out_specs=out_specs,
        out_shape=out_shape,
        scratch_shapes=[pltpu.VMEM((k, tn), BF16), pltpu.VMEM((k, tn), BF16)],
        compiler_params=_params(("arbitrary", "arbitrary")),
        name="swiglu_up",
    )(*args)


def _proj_kernel(*refs, n_out, ropes, has_copy):
    has_rope = any(r is not None for r in ropes)
    refs = list(refs)
    a_ref, as_ref = refs[:2]
    w_refs = refs[2:2 + n_out]
    pos = 2 + n_out
    if has_rope:
        cos_ref, sin_ref, cos_s_ref, sin_s_ref = refs[pos:pos + 4]
        pos += 4
    if has_copy:
        wsrc_ref = refs[pos]
        pos += 1
    o_refs = refs[pos:pos + n_out]
    os_refs = refs[pos + n_out:pos + 2 * n_out]
    pos += 2 * n_out
    if has_copy:
        wcopy_ref = refs[pos]
        pos += 1
    w_bfs = refs[pos:]
    first = pl.program_id(1) == 0
    tm = a_ref.shape[0]

    def outputs(a, cos, sin):
        ys = []
        for w_bf, rope_scale in zip(w_bfs, ropes):
            y = jnp.dot(a, w_bf[...], preferred_element_type=F32)
            if rope_scale is not None:
                half = cos.shape[-1]
                parts = []
                for hd in range(y.shape[-1] // (2 * half)):
                    x1 = y[:, hd * 2 * half:hd * 2 * half + half]
                    x2 = y[:, hd * 2 * half + half:(hd + 1) * 2 * half]
                    parts += [(x1 * cos - x2 * sin) * rope_scale, (x1 * sin + x2 * cos) * rope_scale]
                y = jnp.concatenate(parts, axis=1)
            ys.append(y)
        return ys

    @pl.when(first)
    def _():
        for w_ref, w_bf in zip(w_refs, w_bfs):
            w_bf[...] = w_ref[...].astype(BF16)
        cos = jnp.concatenate([cos_ref[...], cos_s_ref[...]], axis=0) if has_rope else None
        sin = jnp.concatenate([sin_ref[...], sin_s_ref[...]], axis=0) if has_rope else None
        ys = outputs(jnp.concatenate([a_ref[...], as_ref[...]], axis=0), cos, sin)
        for y, o_ref, os_ref in zip(ys, o_refs, os_refs):
            o_ref[...] = y[:tm].astype(o_ref.dtype)
            os_ref[...] = y[tm:].astype(os_ref.dtype)

    @pl.when(jnp.logical_not(first))
    def _():
        ys = outputs(a_ref[...], cos_ref[...] if has_rope else None, sin_ref[...] if has_rope else None)
        for y, o_ref in zip(ys, o_refs):
            o_ref[...] = y.astype(o_ref.dtype)

    if has_copy:
        wcopy_ref[...] = wsrc_ref[...].astype(BF16)


def project(h, hs, w, outs, ncols, tn, rope=None, rope_s=None, w_copy_src=None):
    m, k = h.shape
    ms = hs.shape[0]
    tm = _row_tile(m, ROW_TILE)
    ncol_steps = ncols // tn
    ropes = tuple(o[2] for o in outs)
    in_specs = [pl.BlockSpec((tm, k), lambda j, i: (i, 0)), pl.BlockSpec((ms, k), lambda j, i: (0, 0))]
    for c0, _, _ in outs:
        in_specs.append(pl.BlockSpec((k, tn), functools.partial(lambda j, i, off: (0, off + j), off=c0 // tn)))
    args = [h, hs] + [w] * len(outs)
    if rope is not None:
        half = rope[0].shape[-1]
        in_specs += ([pl.BlockSpec((tm, half), lambda j, i: (i, 0))] * 2
                     + [pl.BlockSpec((ms, half), lambda j, i: (0, 0))] * 2)
        args += list(rope) + list(rope_s)
    out_specs = ([pl.BlockSpec((tm, tn), lambda j, i: (i, j))] * len(outs)
                 + [pl.BlockSpec((ms, tn), lambda j, i: (0, j))] * len(outs))
    out_shape = ([jax.ShapeDtypeStruct((m, ncols), o[1]) for o in outs]
                 + [jax.ShapeDtypeStruct((ms, ncols), F32) for _ in outs])
    if w_copy_src is not None:
        r, d = w_copy_src.shape
        n_row = m // tm
        slab = r // (ncol_steps * n_row)
        in_specs.append(pl.BlockSpec((slab, d), lambda j, i: (j * n_row + i, 0)))
        args.append(w_copy_src)
        out_specs.append(pl.BlockSpec((slab, d), lambda j, i: (j * n_row + i, 0)))
        out_shape.append(jax.ShapeDtypeStruct((r, d), BF16))
    res = pl.pallas_call(
        functools.partial(_proj_kernel, n_out=len(outs), ropes=ropes, has_copy=w_copy_src is not None),
        grid=(ncol_steps, m // tm),
        in_specs=in_specs,
        out_specs=out_specs,
        out_shape=out_shape,
        scratch_shapes=[pltpu.VMEM((k, tn), BF16)] * len(outs),
        compiler_params=_params(("arbitrary", "arbitrary")),
        name="project_rope" if rope is not None else "project",
    )(*args)
    n = len(outs)
    return res[:n], res[n:2 * n], (res[2 * n] if w_copy_src is not None else None)


def _down_kernel(*refs, res_scale, nk, n_tiles, n_chunks, has_next, row_splits):
    if has_next:
        (a_ref, as_ref, w_ref, x_ref, xs_ref, gpost_ref, gnext_ref,
         xo_ref, xso_ref, ho_ref, hso_ref, acc, acc_s) = refs
    else:
        a_ref, as_ref, w_ref, x_ref, xs_ref, gpost_ref, xo_ref, xso_ref, acc, acc_s = refs
        gnext_ref = ho_ref = hso_ref = None
    i = pl.program_id(0)
    kk = pl.program_id(1)
    tm = acc.shape[1]
    rows = tm // row_splits
    chunk = tm // n_chunks
    slot = lax.rem(i, 2)
    has_matmul = i < n_tiles

    @pl.when(jnp.logical_and(i == 0, kk == 0))
    def _():
        acc[...] = jnp.zeros_like(acc)
        acc_s[...] = jnp.zeros_like(acc_s)

    @pl.when(jnp.logical_and(jnp.logical_and(i > 0, has_matmul), kk == 0))
    def _():
        acc[slot] = jnp.zeros(acc.shape[1:], acc.dtype)

    def accumulate(with_sample_rows):
        w = w_ref[...]
        for r in range(row_splits):
            sl = slice(r * rows, (r + 1) * rows)
            a = a_ref[sl, :].astype(BF16)
            if with_sample_rows and r == row_splits - 1:
                d = jnp.dot(jnp.concatenate([a, as_ref[...].astype(BF16)], axis=0), w, preferred_element_type=F32)
                acc[slot, sl, :] += d[:rows]
                acc_s[...] += d[rows:]
            else:
                acc[slot, sl, :] += jnp.dot(a, w, preferred_element_type=F32)

    def finish(x, total, o_ref, h_ref):
        xn = x + res_scale * _rms(total, gpost_ref[...])
        o_ref[...] = xn
        if has_next:
            h_ref[...] = _rms(xn, gnext_ref[...]).astype(h_ref.dtype)

    def finish_chunk():
        r0 = pl.multiple_of(jnp.minimum(kk, n_chunks - 1) * chunk, chunk)
        finish(x_ref[...], acc[1 - slot, pl.ds(r0, chunk), :], xo_ref, ho_ref)

    @pl.when(i == 0)
    def _():
        finish_chunk()
        accumulate(True)

    @pl.when(jnp.logical_and(i > 0, has_matmul))
    def _():
        finish_chunk()
        accumulate(False)

    @pl.when(jnp.logical_not(has_matmul))
    def _():
        finish_chunk()

    @pl.when(jnp.logical_and(i == 1, kk == 0))
    def _():
        finish(xs_ref[...], acc_s[...], xso_ref, hso_ref)


def down_residual(a, a_s, w, widx, x, x_s, g_post, g_next, res_scale):
    tiled = a.ndim == 3
    if tiled:
        assert a.shape[2] == K_TILE and a_s.shape[2] == K_TILE
        m, k, ms = a.shape[1], a.shape[0] * K_TILE, a_s.shape[1]
    else:
        (m, k), ms = a.shape, a_s.shape[0]
    d = w.shape[-1]
    tm, tk = _row_tile(m, DOWN_ROW_TILE), K_TILE
    row_splits = DOWN_ROW_SPLITS if tm % (16 * DOWN_ROW_SPLITS) == 0 else 1
    nk = k // tk
    n_tiles = m // tm
    n_chunks = max(c for c in (1, 2, 4, 8) if c <= nk and tm % (16 * c) == 0)
    chunk = tm // n_chunks
    i0, i1 = widx
    has_next = g_next is not None

    def chunk_map(i, kk):
        return (jnp.where(i == 0, 0, (i - 1) * n_chunks + jnp.minimum(kk, n_chunks - 1)), 0)

    crow = pl.BlockSpec((chunk, d), chunk_map)
    srow = pl.BlockSpec((ms, d), lambda i, kk: (0, 0))
    vec = pl.BlockSpec((1, d), lambda i, kk: (0, 0))
    if tiled:
        a_specs = [pl.BlockSpec((None, tm, tk), lambda i, kk: (kk, jnp.minimum(i, n_tiles - 1), 0)),
                   pl.BlockSpec((None, ms, tk), lambda i, kk: (kk, 0, 0))]
    else:
        a_specs = [pl.BlockSpec((tm, tk), lambda i, kk: (jnp.minimum(i, n_tiles - 1), kk)),
                   pl.BlockSpec((ms, tk), lambda i, kk: (0, kk))]
    in_specs = a_specs + [pl.BlockSpec((None, None, tk, d), lambda i, kk: (i0, i1, kk, 0)), crow, srow, vec]
    args = [a, a_s, w, x, x_s, g_post.reshape(1, d)]
    out_specs = [crow, srow]
    out_shape = [jax.ShapeDtypeStruct((m, d), F32), jax.ShapeDtypeStruct((ms, d), F32)]
    if has_next:
        in_specs.append(vec)
        args.append(g_next.reshape(1, d))
        out_specs += [crow, srow]
        out_shape += [jax.ShapeDtypeStruct((m, d), BF16), jax.ShapeDtypeStruct((ms, d), BF16)]
    outs = pl.pallas_call(
        functools.partial(_down_kernel, res_scale=res_scale, nk=nk, n_tiles=n_tiles, n_chunks=n_chunks,
                          has_next=has_next, row_splits=row_splits),
        grid=(n_tiles + 1, nk),
        in_specs=in_specs,
        out_specs=out_specs,
        out_shape=out_shape,
        scratch_shapes=[pltpu.VMEM((2, tm, d), F32), pltpu.VMEM((ms, d), F32)],
        compiler_params=_params(("arbitrary", "arbitrary")),
        name="down_residual",
    )(*args)
    return tuple(outs) if has_next else (outs[0], outs[1], None, None)


def _softplus(z):
    return jnp.maximum(z, 0.0) + jnp.log(1.0 + jnp.exp(-jnp.abs(z)))


def _suffix_sum(sp, tri):
    hi = sp.astype(BF16)
    lo = (sp - hi.astype(F32)).astype(BF16)
    return jnp.dot(hi, tri, preferred_element_type=F32) + jnp.dot(lo, tri, preferred_element_type=F32)


def _prompt_strips(q_ref, k_bf, v_bf, acc, run, bias, lo, hi, *, blk, scale):
    row = lax.broadcasted_iota(jnp.int32, (blk, blk), 0)
    col = lax.broadcasted_iota(jnp.int32, (blk, blk), 1)
    tri = (row >= col).astype(BF16)
    visible = col < row

    def mask_diagonal(x):
        top = jnp.where(visible, x[:blk], 0.0)
        return top if x.shape[0] == blk else jnp.concatenate([top, x[blk:]], axis=0)

    for j in reversed(range(lo, hi)):
        r0 = j * blk
        z = lax.dot_general(q_ref[r0:, :], k_bf[r0:r0 + blk, :], (((1,), (1,)), ((), ())),
                            preferred_element_type=F32) * scale + bias
        local = _suffix_sum(mask_diagonal(_softplus(z)), tri)
        p = mask_diagonal(jnp.exp(z - local))
        pv = jnp.dot(p.astype(BF16), v_bf[r0:r0 + blk, :], preferred_element_type=F32)
        acc[r0:, :] += jnp.exp(-run[r0:, :]) * pv
        run[r0:, :] += local[:, 0:1]


def _sb_prompt_kernel(bias_ref, q_ref, k_ref, v_ref, o_ref, k_bf, v_bf, acc, run, *, blk, scale):
    k_bf[...] = k_ref[...].astype(BF16)
    v_bf[...] = v_ref[...].astype(BF16)
    acc[...] = jnp.zeros_like(acc)
    run[...] = jnp.zeros_like(run)
    _prompt_strips(q_ref, k_bf, v_bf, acc, run, bias_ref[pl.program_id(1)], 0, q_ref.shape[0] // blk,
                   blk=blk, scale=scale)
    o_ref[...] = acc[...].astype(o_ref.dtype)


def sb_attention_prompt(q, k, v, bias, batch, seq):
    dh = SB_HEAD_DIM
    spec = pl.BlockSpec((seq, dh), lambda b, h: (b, h))
    return pl.pallas_call(
        functools.partial(_sb_prompt_kernel, blk=ATT_BLOCK, scale=dh ** -0.5),
        grid=(batch, SB_HEADS),
        in_specs=[pl.BlockSpec(memory_space=pltpu.SMEM), spec, spec, spec],
        out_specs=spec,
        out_shape=jax.ShapeDtypeStruct(q.shape, BF16),
        scratch_shapes=[pltpu.VMEM((seq, dh), BF16), pltpu.VMEM((seq, dh), BF16),
                        pltpu.VMEM((seq, dh), F32), pltpu.VMEM((seq, 1), F32)],
        compiler_params=_params(("arbitrary", "arbitrary")),
        name="sb_attention_prompt",
    )(bias, q, k, v)


def _decode_pages(q, bias, k_refs, v_refs, acc, run, z_rows, a_flat, g_rows, *, scale):
    page, heads, dh = k_refs[0].shape
    flat = page * heads
    n_groups = flat // LANES
    lane = lax.broadcasted_iota(jnp.int32, (heads, flat), 1)
    own = lax.rem(lane, heads) == lax.broadcasted_iota(jnp.int32, (heads, flat), 0)
    src = lax.broadcasted_iota(jnp.int32, (LANES, 2 * LANES), 0)
    dst = lax.broadcasted_iota(jnp.int32, (LANES, 2 * LANES), 1)
    same_head = lax.rem(src, heads) == lax.rem(dst, heads)
    newer = (src // heads) >= (lax.rem(dst, LANES) // heads)
    sel = (same_head & (newer | (dst >= LANES))).astype(BF16)

    def logits(g):
        kf = k_refs[g][...].reshape(flat, dh).astype(BF16)
        zt = lax.dot_general(q, kf, (((1,), (1,)), ((), ())), preferred_element_type=F32)
        zf = jnp.sum(jnp.where(own, zt, 0.0), axis=0, keepdims=True)
        for c in range(n_groups):
            z_rows[g, c:c + 1, :] = zf[:, c * LANES:(c + 1) * LANES]
        z = z_rows[g] * scale + bias
        both = _suffix_sum(_softplus(z), sel)
        local, group_total = both[:, :LANES], both[:, LANES:]
        newer_groups = jnp.zeros((1, LANES), F32)
        for c in range(n_groups - 1, -1, -1):
            g_rows[g, c:c + 1, :] = newer_groups
            newer_groups = newer_groups + group_total[c:c + 1, :]
        return z - (local + g_rows[g]), newer_groups

    pages = [logits(g) for g in range(len(k_refs))]
    seen = run[...]
    for g, (log_a, page_total) in enumerate(pages):
        a = jnp.exp(log_a - seen)
        seen = seen + page_total
        for c in range(n_groups):
            a_flat[g, :, c * LANES:(c + 1) * LANES] = a[c:c + 1, :]
        a_own = jnp.where(own, jnp.broadcast_to(a_flat[g], (heads, flat)), 0.0).astype(BF16)
        vf = v_refs[g][...].reshape(flat, dh).astype(BF16)
        acc[...] += jnp.dot(a_own, vf, preferred_element_type=F32)
    run[...] = seen


def _decode_scratch(group, page, heads, dh):
    n_groups = page * heads // LANES
    return [pltpu.VMEM((heads, dh), F32), pltpu.VMEM((1, LANES), F32),
            pltpu.VMEM((group, n_groups, LANES), F32), pltpu.VMEM((group, 1, page * heads), F32),
            pltpu.VMEM((group, n_groups, LANES), F32)]


def _sb_decode_kernel(pt_ref, q_ref, bias_ref, *refs, scale, n_steps, group):
    del pt_ref
    k_refs, v_refs = refs[:group], refs[group:2 * group]
    o_ref, acc, run, z_rows, a_flat, g_rows = refs[2 * group:]
    p = pl.program_id(1)

    @pl.when(p == 0)
    def _():
        acc[...] = jnp.zeros_like(acc)
        run[...] = jnp.zeros_like(run)

    _decode_pages(q_ref[...].astype(BF16), bias_ref[...], k_refs, v_refs, acc, run, z_rows, a_flat, g_rows,
                  scale=scale)

    @pl.when(p == n_steps - 1)
    def _():
        o_ref[...] = acc[...]


def sb_attention_decode(q, k_pool, v_pool, layer, page_table, bias):
    b, heads, dh = q.shape
    n_pages = page_table.shape[1]
    page = k_pool.shape[2]
    group = max(g for g in range(1, DECODE_PAGES_PER_STEP + 1) if n_pages % g == 0)
    n_steps = n_pages // group

    def pool_spec(g):
        def page_map(bi, p, pt):
            return (layer, pt[bi * n_pages + (n_pages - 1 - (p * group + g))], 0, 0, 0)
        return pl.BlockSpec((None, None, page, heads, dh), page_map)

    pool_specs = [pool_spec(g) for g in range(group)]
    q_spec = pl.BlockSpec((None, heads, dh), lambda bi, p, pt: (bi, 0, 0))
    bias_lanes = jnp.tile(bias, LANES // heads).reshape(1, LANES)
    return pl.pallas_call(
        functools.partial(_sb_decode_kernel, scale=dh ** -0.5, n_steps=n_steps, group=group),
        grid_spec=pltpu.PrefetchScalarGridSpec(
            num_scalar_prefetch=1,
            grid=(b, n_steps),
            in_specs=[q_spec, pl.BlockSpec((1, LANES), lambda bi, p, pt: (0, 0))] + pool_specs + pool_specs,
            out_specs=q_spec,
            scratch_shapes=_decode_scratch(group, page, heads, dh),
        ),
        out_shape=jax.ShapeDtypeStruct((b, heads, dh), F32),
        compiler_params=_params(("arbitrary", "arbitrary")),
        name="sb_attention_decode",
    )(page_table.reshape(-1), q, bias_lanes, *([k_pool] * group), *([v_pool] * group))


def _sb_fused_kernel(pt_ref, bias_ref, q_ref, k_ref, v_ref, qd_ref, biasd_ref, *refs,
                     blk, scale, group, steps_per_seq):
    del pt_ref
    k_refs, v_refs = refs[:group], refs[group:2 * group]
    o_ref, od_ref, k_bf, v_bf, acc, run, acc_d, run_d, z_rows, a_flat, g_rows = refs[2 * group:]
    h = pl.program_id(1)
    half = pl.program_id(2)
    nblk = q_ref.shape[0] // blk
    step = (pl.program_id(0) * pl.num_programs(1) + h) * 2 + half
    t = lax.rem(step, steps_per_seq)
    bias = bias_ref[h]

    @pl.when(t == 0)
    def _():
        acc_d[...] = jnp.zeros_like(acc_d)
        run_d[...] = jnp.zeros_like(run_d)

    def decode():
        _decode_pages(qd_ref[...].astype(BF16), biasd_ref[...], k_refs, v_refs, acc_d, run_d, z_rows, a_flat,
                      g_rows, scale=scale)

    @pl.when(half == 0)
    def _():
        k_bf[...] = k_ref[...].astype(BF16)
        v_bf[...] = v_ref[...].astype(BF16)
        acc[...] = jnp.zeros_like(acc)
        run[...] = jnp.zeros_like(run)
        decode()
        _prompt_strips(q_ref, k_bf, v_bf, acc, run, bias, nblk // 2, nblk, blk=blk, scale=scale)

    @pl.when(half == 1)
    def _():
        decode()
        _prompt_strips(q_ref, k_bf, v_bf, acc, run, bias, 0, nblk // 2, blk=blk, scale=scale)
        o_ref[...] = acc[...].astype(o_ref.dtype)

    @pl.when(t == steps_per_seq - 1)
    def _():
        od_ref[...] = acc_d[...]


def sb_attention_fused(q, k, v, bias, batch, seq, q_d, k_pool, v_pool, layer, page_table):
    dh = SB_HEAD_DIM
    b_d, heads, _ = q_d.shape
    n_pages = page_table.shape[1]
    page = k_pool.shape[2]
    n_half_steps = 2 * batch * heads
    steps_per_seq = n_half_steps // b_d
    group = n_pages // steps_per_seq

    def pool_spec(g):
        def page_map(b, h, half, pt):
            step = (b * heads + h) * 2 + half
            return (layer, pt[(step // steps_per_seq) * n_pages
                              + (n_pages - 1 - ((step % steps_per_seq) * group + g))], 0, 0, 0)
        return pl.BlockSpec((None, None, page, heads, dh), page_map)

    pool_specs = [pool_spec(g) for g in range(group)]
    spec = pl.BlockSpec((seq, dh), lambda b, h, half, pt: (b, h))
    qd_spec = pl.BlockSpec((None, heads, dh), lambda b, h, half, pt: (((b * heads + h) * 2) // steps_per_seq, 0, 0))
    bias_lanes = jnp.tile(bias, LANES // heads).reshape(1, LANES)
    return pl.pallas_call(
        functools.partial(_sb_fused_kernel, blk=ATT_BLOCK, scale=dh ** -0.5, group=group,
                          steps_per_seq=steps_per_seq),
        grid_spec=pltpu.PrefetchScalarGridSpec(
            num_scalar_prefetch=1,
            grid=(batch, heads, 2),
            in_specs=[pl.BlockSpec(memory_space=pltpu.SMEM), spec, spec, spec, qd_spec,
                      pl.BlockSpec((1, LANES), lambda b, h, half, pt: (0, 0))] + pool_specs + pool_specs,
            out_specs=[spec, qd_spec],
            scratch_shapes=[pltpu.VMEM((seq, dh), BF16), pltpu.VMEM((seq, dh), BF16),
                            pltpu.VMEM((seq, dh), F32), pltpu.VMEM((seq, 1), F32)]
            + _decode_scratch(group, page, heads, dh),
        ),
        out_shape=[jax.ShapeDtypeStruct(q.shape, BF16), jax.ShapeDtypeStruct((b_d, heads, dh), F32)],
        compiler_params=_params(("arbitrary", "arbitrary", "arbitrary")),
        name="sb_attention_fused",
    )(page_table.reshape(-1), bias, q, k, v, q_d, bias_lanes, *([k_pool] * group), *([v_pool] * group))


def _can_fuse_attention(batch, heads, dec_batch, n_pages):
    n_half_steps = 2 * batch * heads
    if n_half_steps % dec_batch:
        return False
    steps_per_seq = n_half_steps // dec_batch
    return steps_per_seq % 2 == 0 and n_pages % steps_per_seq == 0 and n_pages // steps_per_seq <= 8


def _group_norm_gate(o, gate, gn):
    mu = jnp.mean(o, axis=-1, keepdims=True)
    var = jnp.mean((o - mu) ** 2, axis=-1, keepdims=True)
    return (o - mu) * lax.rsqrt(var + EPS) * gn * (gate * jax.nn.sigmoid(gate))


def _ret_prompt_kernel(lg_ref, q_ref, k_ref, v_ref, g_ref, gn_ref, o_ref, s_ref, state, *, chunk, n_chunks):
    lg = lg_ref[pl.program_id(1)]
    n = lax.broadcasted_iota(jnp.int32, (chunk, chunk), 0)
    mcol = lax.broadcasted_iota(jnp.int32, (chunk, chunk), 1)
    diff = (n - mcol).astype(F32)
    decay = jnp.where(diff >= 0, jnp.exp(lg * jnp.maximum(diff, 0.0)), 0.0)
    idx = lax.broadcasted_iota(jnp.int32, (chunk, 1), 0).astype(F32)
    xi = jnp.exp(lg * (idx + 1.0))
    zeta = jnp.exp(lg * (chunk - 1.0 - idx))
    chunk_decay = jnp.exp(lg * jnp.full((1, state.shape[1]), float(chunk), F32))
    gn = gn_ref[...]

    state[...] = jnp.zeros_like(state)

    for c in range(n_chunks):
        rows = slice(c * chunk, (c + 1) * chunk)
        qc, kc, vc = q_ref[rows, :], k_ref[rows, :], v_ref[rows, :]
        s0 = state[...]
        scores = lax.dot_general(qc, kc, (((1,), (1,)), ((), ())), preferred_element_type=F32) * decay
        inner = jnp.dot(scores.astype(BF16), vc, preferred_element_type=F32)
        cross = jnp.dot(qc, s0.astype(BF16), preferred_element_type=F32) * xi
        kz = (kc.astype(F32) * zeta).astype(BF16)
        state[...] = chunk_decay * s0 + lax.dot_general(kz, vc, (((0,), (0,)), ((), ())),
                                                        preferred_element_type=F32)
        o_ref[rows, :] = _group_norm_gate(inner + cross, g_ref[rows, :], gn).astype(o_ref.dtype)
    s_ref[...] = state[...]


def retention_prompt(q, k, v, g, gn, log_gamma, batch, seq):
    dk, dv = RET_QK_DIM, RET_V_DIM
    qk_spec = pl.BlockSpec((seq, dk), lambda b, h: (b, h))
    v_spec = pl.BlockSpec((seq, dv), lambda b, h: (b, h))
    return pl.pallas_call(
        functools.partial(_ret_prompt_kernel, chunk=RET_CHUNK, n_chunks=seq // RET_CHUNK),
        grid=(batch, RET_HEADS),
        in_specs=[pl.BlockSpec(memory_space=pltpu.SMEM), qk_spec, qk_spec, v_spec, v_spec,
                  pl.BlockSpec((1, dv), lambda b, h: (0, h))],
        out_specs=[v_spec, pl.BlockSpec((None, None, dk, dv), lambda b, h: (b, h, 0, 0))],
        out_shape=[jax.ShapeDtypeStruct(v.shape, BF16),
                   jax.ShapeDtypeStruct((batch, RET_HEADS, dk, dv), F32)],
        scratch_shapes=[pltpu.VMEM((dk, dv), F32)],
        compiler_params=_params(("arbitrary", "arbitrary")),
        name="retention_prompt",
    )(log_gamma, q, k, v, g, gn.reshape(1, -1))


def _ret_decode_kernel(lg_ref, q_ref, k_ref, v_ref, g_ref, gn_ref, s0_ref, o_ref, s_ref):
    lg = lg_ref[pl.program_id(1)]
    dv = v_ref.shape[-1]
    gamma = jnp.exp(lg * jnp.ones((1, dv), F32))
    q, k, v, s0 = q_ref[...], k_ref[...], v_ref[...], s0_ref[...]
    qk = jnp.sum(q * k, axis=0, keepdims=True)
    cross = jnp.sum(q * s0, axis=0, keepdims=True) * gamma
    o = qk * v + cross
    s_ref[...] = gamma * s0 + k * v
    o_ref[...] = _group_norm_gate(o, g_ref[...], gn_ref[...]).astype(o_ref.dtype)


def retention_decode(q, k, v, g, gn, log_gamma, state):
    b = q.shape[0]
    h, dk, dv = RET_HEADS, RET_QK_DIM, RET_V_DIM
    col_spec = pl.BlockSpec((None, None, dk, 1), lambda bi, hi: (bi, hi, 0, 0))
    row_spec = pl.BlockSpec((None, None, 1, dv), lambda bi, hi: (bi, hi, 0, 0))
    st_spec = pl.BlockSpec((None, None, dk, dv), lambda bi, hi: (bi, hi, 0, 0))
    o, s_new = pl.pallas_call(
        _ret_decode_kernel,
        grid=(b, h),
        in_specs=[pl.BlockSpec(memory_space=pltpu.SMEM), col_spec, col_spec, row_spec, row_spec,
                  pl.BlockSpec((None, 1, dv), lambda bi, hi: (hi, 0, 0)), st_spec],
        out_specs=[row_spec, st_spec],
        out_shape=[jax.ShapeDtypeStruct((b, h, 1, dv), BF16), jax.ShapeDtypeStruct(state.shape, F32)],
        compiler_params=_params(("arbitrary", "arbitrary")),
        name="retention_decode",
    )(log_gamma, q.reshape(b, h, dk, 1), k.reshape(b, h, dk, 1), v.reshape(b, h, 1, dv),
      g.reshape(b, h, 1, dv), gn.reshape(h, 1, dv), state)
    return o.reshape(b, h * dv), s_new


def _stack(xs):
    return xs[0][None] if len(xs) == 1 else jnp.stack(xs)


def _rope_tables(pos, rows):
    half = RET_QK_DIM // 2
    inv = ROPE_BASE ** (-jnp.arange(half, dtype=F32) / half)
    ang = pos.astype(F32)[:, None] * inv[None, :]
    cos, sin = jnp.cos(ang), jnp.sin(ang)
    reps = rows // pos.shape[0]
    return jnp.tile(cos, (reps, 1)), jnp.tile(sin, (reps, 1))


SAMPLE_ROWS = 16


def kernel(x_prompt, x_sample, cache_k, cache_v, state_ret, page_table, norm_g, ffn_w_gate, ffn_w_up, ffn_w_down,
           sb_w_qkv, sb_w_o, sb_bias, ret_w_qkvg, ret_gn_g, ret_w_o):
    batch, seq, d = x_prompt.shape
    dec_batch, dec_seq, _ = x_sample.shape
    depth = norm_g.shape[0]
    past_len = page_table.shape[1] * PAGE_SIZE
    assert dec_seq == 1, "the decode kernels take one new token per sequence"
    mp, ms = batch * seq, dec_batch * dec_seq
    assert ms <= SAMPLE_ROWS

    def pad_rows(t):
        return jnp.concatenate([t, jnp.zeros((SAMPLE_ROWS - ms, t.shape[1]), t.dtype)], axis=0)

    log_gamma = jnp.log1p(-jnp.exp2(-5.0 - jnp.arange(RET_HEADS, dtype=F32)))
    rope_p = _rope_tables(jnp.arange(seq, dtype=jnp.int32), mp)
    rope_s = _rope_tables(past_len + jnp.arange(dec_seq, dtype=jnp.int32), ms)
    rope_s = (pad_rows(rope_s[0]), pad_rows(rope_s[1]))

    xp, xs = x_prompt.reshape(mp, d), pad_rows(x_sample.reshape(ms, d))
    hp, hs = prenorm(xp, norm_g[0, 0]), prenorm(xs, norm_g[0, 0])
    kp, vp, ksm, vsm, rsp, rss = [], [], [], [], [], []

    for i in range(depth):
        g = norm_g[i]
        layer = i // 2

        def ffn(x, x_s, h, h_s, j, g_post, g_next):
            act, act_s, w_down = swiglu_up(h, h_s, ffn_w_gate, ffn_w_up, (i, j), w_copy_src=ffn_w_down)
            return down_residual(act, act_s, w_down[None, None], (0, 0), x, x_s, g_post, g_next, 0.5)

        xp, xs, hp, hs = ffn(xp, xs, hp, hs, 0, g[1], g[2])

        if i % 2 == 0:
            w = sb_w_qkv[layer]
            bias = sb_bias[layer]
            (q, k, v), (q_s, k_s, v_s), w_o = project(
                hp, hs, w, [(0, BF16, None), (d, F32, None), (2 * d, F32, None)], d, QKV_COL_TILE,
                w_copy_src=sb_w_o[layer])
            q_d = q_s[:ms].reshape(ms, SB_HEADS, SB_HEAD_DIM)
            if _can_fuse_attention(batch, SB_HEADS, ms, page_table.shape[1]):
                mix_p, mix_s = sb_attention_fused(q, k, v, bias, batch, seq, q_d, cache_k, cache_v, layer, page_table)
            else:
                mix_p = sb_attention_prompt(q, k, v, bias, batch, seq)
                mix_s = sb_attention_decode(q_d, cache_k, cache_v, layer, page_table, bias)
            mix_s = mix_s.reshape(ms, d)
            kp.append(k.reshape(batch, seq, SB_HEADS, SB_HEAD_DIM))
            vp.append(v.reshape(batch, seq, SB_HEADS, SB_HEAD_DIM))
            ksm.append(k_s[:ms].reshape(dec_batch, dec_seq, SB_HEADS, SB_HEAD_DIM))
            vsm.append(v_s[:ms].reshape(dec_batch, dec_seq, SB_HEADS, SB_HEAD_DIM))
        else:
            w = ret_w_qkvg[layer]
            gn = ret_gn_g[layer]
            (q, k), (q_s, k_s), _ = project(hp, hs, w, [(0, BF16, RET_QK_DIM ** -0.5), (d, BF16, 1.0)], d, COL_TILE,
                                            rope=rope_p, rope_s=rope_s)
            (v, gate), (v_s, gate_s), w_o = project(hp, hs, w, [(2 * d, BF16, None), (4 * d, F32, None)], 2 * d,
                                                    COL_TILE, w_copy_src=ret_w_o[layer])
            mix_p, st_p = retention_prompt(q, k, v, gate, gn, log_gamma, batch, seq)
            rsp.append(st_p)
            mix_s, st_s = retention_decode(q_s[:ms], k_s[:ms], v_s[:ms], gate_s[:ms], gn, log_gamma,
                                           state_ret[layer])
            rss.append(st_s)

        xp, xs, hp, hs = down_residual(mix_p, pad_rows(mix_s), w_o[None, None], (0, 0), xp, xs, g[3], g[4], 1.0)

        g_next = norm_g[i + 1, 0] if i + 1 < depth else None
        xp, xs, hp, hs = ffn(xp, xs, hp, hs, 1, g[5], g_next)

    return (xp.reshape(batch, seq, d), xs[:ms].reshape(dec_batch, dec_seq, d), _stack(kp), _stack(vp),
            _stack(ksm), _stack(vsm), _stack(rsp), _stack(rss))
```

```python
import functools

import jax
import jax.numpy as jnp
from jax import lax
from jax.experimental import pallas as pl
from jax.experimental.pallas import tpu as pltpu

F32 = jnp.float32
BF16 = jnp.bfloat16

EPS = 1e-6
SB_HEADS = 16
SB_HEAD_DIM = 128
PAGE_SIZE = 128
RET_HEADS = 8
RET_QK_DIM = 256
RET_V_DIM = 512
RET_CHUNK = 128
ROPE_BASE = 10000.0

VMEM_LIMIT_BYTES = 60 * 1024 * 1024
LANES = 128

ROW_TILE = 1024
DOWN_ROW_TILE = 1024
DOWN_ROW_SPLITS = 2
COL_TILE = 512
QKV_COL_TILE = 256
MAX_K_TILE = 1408
ATT_BLOCK = 256
DECODE_PAGES_PER_STEP = 4


def _params(sem):
    return pltpu.CompilerParams(dimension_semantics=sem, vmem_limit_bytes=VMEM_LIMIT_BYTES)


def _rms(x, g):
    return x * lax.rsqrt(jnp.mean(x * x, axis=-1, keepdims=True) + EPS) * g


def _row_tile(m, want):
    return want if m % want == 0 else m


def _prenorm_kernel(x_ref, g_ref, o_ref):
    o_ref[...] = _rms(x_ref[...], g_ref[...]).astype(o_ref.dtype)


def prenorm(x, g):
    m, d = x.shape
    tm = _row_tile(m, DOWN_ROW_TILE)
    return pl.pallas_call(
        _prenorm_kernel,
        grid=(m // tm,),
        in_specs=[pl.BlockSpec((tm, d), lambda i: (i, 0)), pl.BlockSpec((1, d), lambda i: (0, 0))],
        out_specs=pl.BlockSpec((tm, d), lambda i: (i, 0)),
        out_shape=jax.ShapeDtypeStruct((m, d), BF16),
        compiler_params=_params(("arbitrary",)),
        name="prenorm",
    )(x, g.reshape(1, d))


def _swiglu_kernel(*refs, has_copy):
    if has_copy:
        a_ref, as_ref, wg_ref, wu_ref, wsrc_ref, o_ref, os_ref, wcopy_ref, wg_bf, wu_bf = refs
    else:
        a_ref, as_ref, wg_ref, wu_ref, o_ref, os_ref, wg_bf, wu_bf = refs
    first = pl.program_id(1) == 0
    tm = a_ref.shape[0]

    def act(a):
        gate = jnp.dot(a, wg_bf[...], preferred_element_type=F32)
        up = jnp.dot(a, wu_bf[...], preferred_element_type=F32)
        return (gate * jax.nn.sigmoid(gate) * up).astype(o_ref.dtype)

    @pl.when(first)
    def _():
        wg_bf[...] = wg_ref[...].astype(BF16)
        wu_bf[...] = wu_ref[...].astype(BF16)
        y = act(jnp.concatenate([a_ref[...], as_ref[...]], axis=0))
        o_ref[...] = y[:tm]
        os_ref[...] = y[tm:]

    @pl.when(jnp.logical_not(first))
    def _():
        o_ref[...] = act(a_ref[...])

    if has_copy:
        wcopy_ref[...] = wsrc_ref[...].astype(BF16)


def swiglu_up(h, hs, w_gate, w_up, widx, w_copy_src=None):
    m, k = h.shape
    n = w_gate.shape[-1]
    tm, tn = _row_tile(m, ROW_TILE), COL_TILE
    i0, i1 = widx
    wspec = pl.BlockSpec((None, None, k, tn), lambda j, i: (i0, i1, 0, j))
    in_specs = [pl.BlockSpec((tm, k), lambda j, i: (i, 0)), pl.BlockSpec(hs.shape, lambda j, i: (0, 0)), wspec, wspec]
    out_specs = [pl.BlockSpec((tm, tn), lambda j, i: (i, j)), pl.BlockSpec((hs.shape[0], tn), lambda j, i: (0, j))]
    out_shape = [jax.ShapeDtypeStruct((m, n), BF16), jax.ShapeDtypeStruct((hs.shape[0], n), BF16)]
    args = [h, hs, w_gate, w_up]
    if w_copy_src is not None:
        r, d = w_copy_src.shape[-2:]
        n_row = m // tm
        slab = r // ((n // tn) * n_row)
        in_specs.append(pl.BlockSpec((None, None, slab, d), lambda j, i: (i0, i1, j * n_row + i, 0)))
        out_specs.append(pl.BlockSpec((slab, d), lambda j, i: (j * n_row + i, 0)))
        out_shape.append(jax.ShapeDtypeStruct((r, d), BF16))
        args.append(w_copy_src)
    return pl.pallas_call(
        functools.partial(_swiglu_kernel, has_copy=w_copy_src is not None),
        grid=(n // tn, m // tm),
        in_specs=in_specs,
        out_specs=out_specs,
        out_shape=out_shape,
        scratch_shapes=[pltpu.VMEM((k, tn), BF16), pltpu.VMEM((k, tn), BF16)],
        compiler_params=_params(("arbitrary", "arbitrary")),
        name="swiglu_up",
    )(*args)


def _proj_kernel(*refs, n_out, ropes, has_copy):
    has_rope = any(r is not None for r in ropes)
    refs = list(refs)
    a_ref, as_ref = refs[:2]
    w_refs = refs[2:2 + n_out]
    pos = 2 + n_out
    if has_rope:
        cos_ref, sin_ref, cos_s_ref, sin_s_ref = refs[pos:pos + 4]
        pos += 4
    if has_copy:
        wsrc_ref = refs[pos]
        pos += 1
    o_refs = refs[pos:pos + n_out]
    os_refs = refs[pos + n_out:pos + 2 * n_out]
    pos += 2 * n_out
    if has_copy:
        wcopy_ref = refs[pos]
        pos += 1
    w_bfs = refs[pos:]
    first = pl.program_id(1) == 0
    tm = a_ref.shape[0]

    def outputs(a, cos, sin):
        ys = []
        for w_bf, rope_scale in zip(w_bfs, ropes):
            y = jnp.dot(a, w_bf[...], preferred_element_type=F32)
            if rope_scale is not None:
                half = cos.shape[-1]
                parts = []
                for hd in range(y.shape[-1] // (2 * half)):
                    x1 = y[:, hd * 2 * half:hd * 2 * half + half]
                    x2 = y[:, hd * 2 * half + half:(hd + 1) * 2 * half]
                    parts += [(x1 * cos - x2 * sin) * rope_scale, (x1 * sin + x2 * cos) * rope_scale]
                y = jnp.concatenate(parts, axis=1)
            ys.append(y)
        return ys

    @pl.when(first)
    def _():
        for w_ref, w_bf in zip(w_refs, w_bfs):
            w_bf[...] = w_ref[...].astype(BF16)
        cos = jnp.concatenate([cos_ref[...], cos_s_ref[...]], axis=0) if has_rope else None
        sin = jnp.concatenate([sin_ref[...], sin_s_ref[...]], axis=0) if has_rope else None
        ys = outputs(jnp.concatenate([a_ref[...], as_ref[...]], axis=0), cos, sin)
        for y, o_ref, os_ref in zip(ys, o_refs, os_refs):
            o_ref[...] = y[:tm].astype(o_ref.dtype)
            os_ref[...] = y[tm:].astype(os_ref.dtype)

    @pl.when(jnp.logical_not(first))
    def _():
        ys = outputs(a_ref[...], cos_ref[...] if has_rope else None, sin_ref[...] if has_rope else None)
        for y, o_ref in zip(ys, o_refs):
            o_ref[...] = y.astype(o_ref.dtype)

    if has_copy:
        wcopy_ref[...] = wsrc_ref[...].astype(BF16)


def project(h, hs, w, outs, ncols, tn, rope=None, rope_s=None, w_copy_src=None):
    m, k = h.shape
    ms = hs.shape[0]
    tm = _row_tile(m, ROW_TILE)
    ncol_steps = ncols // tn
    ropes = tuple(o[2] for o in outs)
    in_specs = [pl.BlockSpec((tm, k), lambda j, i: (i, 0)), pl.BlockSpec((ms, k), lambda j, i: (0, 0))]
    for c0, _, _ in outs:
        in_specs.append(pl.BlockSpec((k, tn), functools.partial(lambda j, i, off: (0, off + j), off=c0 // tn)))
    args = [h, hs] + [w] * len(outs)
    if rope is not None:
        half = rope[0].shape[-1]
        in_specs += ([pl.BlockSpec((tm, half), lambda j, i: (i, 0))] * 2
                     + [pl.BlockSpec((ms, half), lambda j, i: (0, 0))] * 2)
        args += list(rope) + list(rope_s)
    out_specs = ([pl.BlockSpec((tm, tn), lambda j, i: (i, j))] * len(outs)
                 + [pl.BlockSpec((ms, tn), lambda j, i: (0, j))] * len(outs))
    out_shape = ([jax.ShapeDtypeStruct((m, ncols), o[1]) for o in outs]
                 + [jax.ShapeDtypeStruct((ms, ncols), F32) for _ in outs])
    if w_copy_src is not None:
        r, d = w_copy_src.shape
        n_row = m // tm
        slab = r // (ncol_steps * n_row)
        in_specs.append(pl.BlockSpec((slab, d), lambda j, i: (j * n_row + i, 0)))
        args.append(w_copy_src)
        out_specs.append(pl.BlockSpec((slab, d), lambda j, i: (j * n_row + i, 0)))
        out_shape.append(jax.ShapeDtypeStruct((r, d), BF16))
    res = pl.pallas_call(
        functools.partial(_proj_kernel, n_out=len(outs), ropes=ropes, has_copy=w_copy_src is not None),
        grid=(ncol_steps, m // tm),
        in_specs=in_specs,
        out_specs=out_specs,
        out_shape=out_shape,
        scratch_shapes=[pltpu.VMEM((k, tn), BF16)] * len(outs),
        compiler_params=_params(("arbitrary", "arbitrary")),
        name="project_rope" if rope is not None else "project",
    )(*args)
    n = len(outs)
    return res[:n], res[n:2 * n], (res[2 * n] if w_copy_src is not None else None)


def _down_kernel(*refs, res_scale, nk, n_tiles, n_chunks, has_next, row_splits):
    if has_next:
        (a_ref, as_ref, w_ref, x_ref, xs_ref, gpost_ref, gnext_ref,
         xo_ref, xso_ref, ho_ref, hso_ref, acc, acc_s) = refs
    else:
        a_ref, as_ref, w_ref, x_ref, xs_ref, gpost_ref, xo_ref, xso_ref, acc, acc_s = refs
        gnext_ref = ho_ref = hso_ref = None
    i = pl.program_id(0)
    kk = pl.program_id(1)
    tm = acc.shape[1]
    rows = tm // row_splits
    chunk = tm // n_chunks
    slot = lax.rem(i, 2)
    has_matmul = i < n_tiles

    @pl.when(jnp.logical_and(i == 0, kk == 0))
    def _():
        acc[...] = jnp.zeros_like(acc)
        acc_s[...] = jnp.zeros_like(acc_s)

    @pl.when(jnp.logical_and(jnp.logical_and(i > 0, has_matmul), kk == 0))
    def _():
        acc[slot] = jnp.zeros(acc.shape[1:], acc.dtype)

    def accumulate(with_sample_rows):
        w = w_ref[...]
        for r in range(row_splits):
            sl = slice(r * rows, (r + 1) * rows)
            a = a_ref[sl, :].astype(BF16)
            if with_sample_rows and r == row_splits - 1:
                d = jnp.dot(jnp.concatenate([a, as_ref[...].astype(BF16)], axis=0), w, preferred_element_type=F32)
                acc[slot, sl, :] += d[:rows]
                acc_s[...] += d[rows:]
            else:
                acc[slot, sl, :] += jnp.dot(a, w, preferred_element_type=F32)

    def finish(x, total, o_ref, h_ref):
        xn = x + res_scale * _rms(total, gpost_ref[...])
        o_ref[...] = xn
        if has_next:
            h_ref[...] = _rms(xn, gnext_ref[...]).astype(h_ref.dtype)

    def finish_chunk():
        r0 = pl.multiple_of(jnp.minimum(kk, n_chunks - 1) * chunk, chunk)
        finish(x_ref[...], acc[1 - slot, pl.ds(r0, chunk), :], xo_ref, ho_ref)

    @pl.when(i == 0)
    def _():
        finish_chunk()
        accumulate(True)

    @pl.when(jnp.logical_and(i > 0, has_matmul))
    def _():
        finish_chunk()
        accumulate(False)

    @pl.when(jnp.logical_not(has_matmul))
    def _():
        finish_chunk()

    @pl.when(jnp.logical_and(i == 1, kk == 0))
    def _():
        finish(xs_ref[...], acc_s[...], xso_ref, hso_ref)


def down_residual(a, a_s, w, widx, x, x_s, g_post, g_next, res_scale):
    m, k = a.shape
    ms = a_s.shape[0]
    d = w.shape[-1]
    tm = _row_tile(m, DOWN_ROW_TILE)
    tk = max(t for t in range(LANES, MAX_K_TILE + 1, LANES) if k % t == 0)
    row_splits = DOWN_ROW_SPLITS if tm % (16 * DOWN_ROW_SPLITS) == 0 else 1
    nk = k // tk
    n_tiles = m // tm
    n_chunks = max(c for c in (1, 2, 4, 8) if c <= nk and tm % (16 * c) == 0)
    chunk = tm // n_chunks
    i0, i1 = widx
    has_next = g_next is not None

    def chunk_map(i, kk):
        return (jnp.where(i == 0, 0, (i - 1) * n_chunks + jnp.minimum(kk, n_chunks - 1)), 0)

    crow = pl.BlockSpec((chunk, d), chunk_map)
    srow = pl.BlockSpec((ms, d), lambda i, kk: (0, 0))
    vec = pl.BlockSpec((1, d), lambda i, kk: (0, 0))
    in_specs = [pl.BlockSpec((tm, tk), lambda i, kk: (jnp.minimum(i, n_tiles - 1), kk)),
                pl.BlockSpec((ms, tk), lambda i, kk: (0, kk)),
                pl.BlockSpec((None, None, tk, d), lambda i, kk: (i0, i1, kk, 0)), crow, srow, vec]
    args = [a, a_s, w, x, x_s, g_post.reshape(1, d)]
    out_specs = [crow, srow]
    out_shape = [jax.ShapeDtypeStruct((m, d), F32), jax.ShapeDtypeStruct((ms, d), F32)]
    if has_next:
        in_specs.append(vec)
        args.append(g_next.reshape(1, d))
        out_specs += [crow, srow]
        out_shape += [jax.ShapeDtypeStruct((m, d), BF16), jax.ShapeDtypeStruct((ms, d), BF16)]
    outs = pl.pallas_call(
        functools.partial(_down_kernel, res_scale=res_scale, nk=nk, n_tiles=n_tiles, n_chunks=n_chunks,
                          has_next=has_next, row_splits=row_splits),
        grid=(n_tiles + 1, nk),
        in_specs=in_specs,
        out_specs=out_specs,
        out_shape=out_shape,
        scratch_shapes=[pltpu.VMEM((2, tm, d), F32), pltpu.VMEM((ms, d), F32)],
        compiler_params=_params(("arbitrary", "arbitrary")),
        name="down_residual",
    )(*args)
    return tuple(outs) if has_next else (outs[0], outs[1], None, None)


def _softplus(z):
    return jnp.maximum(z, 0.0) + jnp.log(1.0 + jnp.exp(-jnp.abs(z)))


def _suffix_sum(sp, tri):
    hi = sp.astype(BF16)
    lo = (sp - hi.astype(F32)).astype(BF16)
    return jnp.dot(hi, tri, preferred_element_type=F32) + jnp.dot(lo, tri, preferred_element_type=F32)


def _prompt_strips(q_ref, k_bf, v_bf, acc, run, bias, lo, hi, *, blk, scale):
    row = lax.broadcasted_iota(jnp.int32, (blk, blk), 0)
    col = lax.broadcasted_iota(jnp.int32, (blk, blk), 1)
    tri = (row >= col).astype(BF16)
    visible = col < row

    def mask_diagonal(x):
        top = jnp.where(visible, x[:blk], 0.0)
        return top if x.shape[0] == blk else jnp.concatenate([top, x[blk:]], axis=0)

    for j in reversed(range(lo, hi)):
        r0 = j * blk
        z = lax.dot_general(q_ref[r0:, :], k_bf[r0:r0 + blk, :], (((1,), (1,)), ((), ())),
                            preferred_element_type=F32) * scale + bias
        local = _suffix_sum(mask_diagonal(_softplus(z)), tri)
        p = mask_diagonal(jnp.exp(z - local))
        pv = jnp.dot(p.astype(BF16), v_bf[r0:r0 + blk, :], preferred_element_type=F32)
        acc[r0:, :] += jnp.exp(-run[r0:, :]) * pv
        run[r0:, :] += local[:, 0:1]


def _sb_prompt_kernel(bias_ref, q_ref, k_ref, v_ref, o_ref, k_bf, v_bf, acc, run, *, blk, scale):
    k_bf[...] = k_ref[...].astype(BF16)
    v_bf[...] = v_ref[...].astype(BF16)
    acc[...] = jnp.zeros_like(acc)
    run[...] = jnp.zeros_like(run)
    _prompt_strips(q_ref, k_bf, v_bf, acc, run, bias_ref[pl.program_id(1)], 0, q_ref.shape[0] // blk,
                   blk=blk, scale=scale)
    o_ref[...] = acc[...].astype(o_ref.dtype)


def sb_attention_prompt(q, k, v, bias, batch, seq):
    dh = SB_HEAD_DIM
    spec = pl.BlockSpec((seq, dh), lambda b, h: (b, h))
    return pl.pallas_call(
        functools.partial(_sb_prompt_kernel, blk=ATT_BLOCK, scale=dh ** -0.5),
        grid=(batch, SB_HEADS),
        in_specs=[pl.BlockSpec(memory_space=pltpu.SMEM), spec, spec, spec],
        out_specs=spec,
        out_shape=jax.ShapeDtypeStruct(q.shape, BF16),
        scratch_shapes=[pltpu.VMEM((seq, dh), BF16), pltpu.VMEM((seq, dh), BF16),
                        pltpu.VMEM((seq, dh), F32), pltpu.VMEM((seq, 1), F32)],
        compiler_params=_params(("arbitrary", "arbitrary")),
        name="sb_attention_prompt",
    )(bias, q, k, v)


def _decode_pages(q, bias, k_refs, v_refs, acc, run, z_rows, a_flat, g_rows, *, scale):
    page, heads, dh = k_refs[0].shape
    flat = page * heads
    n_groups = flat // LANES
    lane = lax.broadcasted_iota(jnp.int32, (heads, flat), 1)
    own = lax.rem(lane, heads) == lax.broadcasted_iota(jnp.int32, (heads, flat), 0)
    src = lax.broadcasted_iota(jnp.int32, (LANES, 2 * LANES), 0)
    dst = lax.broadcasted_iota(jnp.int32, (LANES, 2 * LANES), 1)
    same_head = lax.rem(src, heads) == lax.rem(dst, heads)
    newer = (src // heads) >= (lax.rem(dst, LANES) // heads)
    sel = (same_head & (newer | (dst >= LANES))).astype(BF16)

    def logits(g):
        kf = k_refs[g][...].reshape(flat, dh).astype(BF16)
        zt = lax.dot_general(q, kf, (((1,), (1,)), ((), ())), preferred_element_type=F32)
        zf = jnp.sum(jnp.where(own, zt, 0.0), axis=0, keepdims=True)
        for c in range(n_groups):
            z_rows[g, c:c + 1, :] = zf[:, c * LANES:(c + 1) * LANES]
        z = z_rows[g] * scale + bias
        both = _suffix_sum(_softplus(z), sel)
        local, group_total = both[:, :LANES], both[:, LANES:]
        newer_groups = jnp.zeros((1, LANES), F32)
        for c in range(n_groups - 1, -1, -1):
            g_rows[g, c:c + 1, :] = newer_groups
            newer_groups = newer_groups + group_total[c:c + 1, :]
        return z - (local + g_rows[g]), newer_groups

    pages = [logits(g) for g in range(len(k_refs))]
    seen = run[...]
    for g, (log_a, page_total) in enumerate(pages):
        a = jnp.exp(log_a - seen)
        seen = seen + page_total
        for c in range(n_groups):
            a_flat[g, :, c * LANES:(c + 1) * LANES] = a[c:c + 1, :]
        a_own = jnp.where(own, jnp.broadcast_to(a_flat[g], (heads, flat)), 0.0).astype(BF16)
        vf = v_refs[g][...].reshape(flat, dh).astype(BF16)
        acc[...] += jnp.dot(a_own, vf, preferred_element_type=F32)
    run[...] = seen


def _decode_scratch(group, page, heads, dh):
    n_groups = page * heads // LANES
    return [pltpu.VMEM((heads, dh), F32), pltpu.VMEM((1, LANES), F32),
            pltpu.VMEM((group, n_groups, LANES), F32), pltpu.VMEM((group, 1, page * heads), F32),
            pltpu.VMEM((group, n_groups, LANES), F32)]


def _sb_decode_kernel(pt_ref, q_ref, bias_ref, *refs, scale, n_steps, group):
    del pt_ref
    k_refs, v_refs = refs[:group], refs[group:2 * group]
    o_ref, acc, run, z_rows, a_flat, g_rows = refs[2 * group:]
    p = pl.program_id(1)

    @pl.when(p == 0)
    def _():
        acc[...] = jnp.zeros_like(acc)
        run[...] = jnp.zeros_like(run)

    _decode_pages(q_ref[...].astype(BF16), bias_ref[...], k_refs, v_refs, acc, run, z_rows, a_flat, g_rows,
                  scale=scale)

    @pl.when(p == n_steps - 1)
    def _():
        o_ref[...] = acc[...]


def sb_attention_decode(q, k_pool, v_pool, layer, page_table, bias):
    b, heads, dh = q.shape
    n_pages = page_table.shape[1]
    page = k_pool.shape[2]
    group = max(g for g in range(1, DECODE_PAGES_PER_STEP + 1) if n_pages % g == 0)
    n_steps = n_pages // group

    def pool_spec(g):
        def page_map(bi, p, pt):
            return (layer, pt[bi * n_pages + (n_pages - 1 - (p * group + g))], 0, 0, 0)
        return pl.BlockSpec((None, None, page, heads, dh), page_map)

    pool_specs = [pool_spec(g) for g in range(group)]
    q_spec = pl.BlockSpec((None, heads, dh), lambda bi, p, pt: (bi, 0, 0))
    bias_lanes = jnp.tile(bias, LANES // heads).reshape(1, LANES)
    return pl.pallas_call(
        functools.partial(_sb_decode_kernel, scale=dh ** -0.5, n_steps=n_steps, group=group),
        grid_spec=pltpu.PrefetchScalarGridSpec(
            num_scalar_prefetch=1,
            grid=(b, n_steps),
            in_specs=[q_spec, pl.BlockSpec((1, LANES), lambda bi, p, pt: (0, 0))] + pool_specs + pool_specs,
            out_specs=q_spec,
            scratch_shapes=_decode_scratch(group, page, heads, dh),
        ),
        out_shape=jax.ShapeDtypeStruct((b, heads, dh), F32),
        compiler_params=_params(("arbitrary", "arbitrary")),
        name="sb_attention_decode",
    )(page_table.reshape(-1), q, bias_lanes, *([k_pool] * group), *([v_pool] * group))


def _sb_fused_kernel(pt_ref, bias_ref, q_ref, k_ref, v_ref, qd_ref, biasd_ref, *refs,
                     blk, scale, group, steps_per_seq):
    del pt_ref
    k_refs, v_refs = refs[:group], refs[group:2 * group]
    o_ref, od_ref, k_bf, v_bf, acc, run, acc_d, run_d, z_rows, a_flat, g_rows = refs[2 * group:]
    h = pl.program_id(1)
    half = pl.program_id(2)
    nblk = q_ref.shape[0] // blk
    step = (pl.program_id(0) * pl.num_programs(1) + h) * 2 + half
    t = lax.rem(step, steps_per_seq)
    bias = bias_ref[h]

    @pl.when(t == 0)
    def _():
        acc_d[...] = jnp.zeros_like(acc_d)
        run_d[...] = jnp.zeros_like(run_d)

    def decode():
        _decode_pages(qd_ref[...].astype(BF16), biasd_ref[...], k_refs, v_refs, acc_d, run_d, z_rows, a_flat,
                      g_rows, scale=scale)

    split = nblk - max(c for c in range(1, nblk + 1) if c * (c + 1) <= nblk * (nblk + 1) // 2)

    @pl.when(half == 0)
    def _():
        k_bf[...] = k_ref[...].astype(BF16)
        v_bf[...] = v_ref[...].astype(BF16)
        acc[...] = jnp.zeros_like(acc)
        run[...] = jnp.zeros_like(run)
        decode()
        _prompt_strips(q_ref, k_bf, v_bf, acc, run, bias, split, nblk, blk=blk, scale=scale)

    @pl.when(half == 1)
    def _():
        decode()
        _prompt_strips(q_ref, k_bf, v_bf, acc, run, bias, 0, split, blk=blk, scale=scale)
        o_ref[...] = acc[...].astype(o_ref.dtype)

    @pl.when(t == steps_per_seq - 1)
    def _():
        od_ref[...] = acc_d[...]


def sb_attention_fused(q, k, v, bias, batch, seq, q_d, k_pool, v_pool, layer, page_table):
    dh = SB_HEAD_DIM
    b_d, heads, _ = q_d.shape
    n_pages = page_table.shape[1]
    page = k_pool.shape[2]
    n_half_steps = 2 * batch * heads
    steps_per_seq = n_half_steps // b_d
    group = n_pages // steps_per_seq

    def pool_spec(g):
        def page_map(b, h, half, pt):
            step = (b * heads + h) * 2 + half
            return (layer, pt[(step // steps_per_seq) * n_pages
                              + (n_pages - 1 - ((step % steps_per_seq) * group + g))], 0, 0, 0)
        return pl.BlockSpec((None, None, page, heads, dh), page_map)

    pool_specs = [pool_spec(g) for g in range(group)]
    spec = pl.BlockSpec((seq, dh), lambda b, h, half, pt: (b, h))
    qd_spec = pl.BlockSpec((None, heads, dh), lambda b, h, half, pt: (((b * heads + h) * 2) // steps_per_seq, 0, 0))
    bias_lanes = jnp.tile(bias, LANES // heads).reshape(1, LANES)
    return pl.pallas_call(
        functools.partial(_sb_fused_kernel, blk=ATT_BLOCK, scale=dh ** -0.5, group=group,
                          steps_per_seq=steps_per_seq),
        grid_spec=pltpu.PrefetchScalarGridSpec(
            num_scalar_prefetch=1,
            grid=(batch, heads, 2),
            in_specs=[pl.BlockSpec(memory_space=pltpu.SMEM), spec, spec, spec, qd_spec,
                      pl.BlockSpec((1, LANES), lambda b, h, half, pt: (0, 0))] + pool_specs + pool_specs,
            out_specs=[spec, qd_spec],
            scratch_shapes=[pltpu.VMEM((seq, dh), BF16), pltpu.VMEM((seq, dh), BF16),
                            pltpu.VMEM((seq, dh), F32), pltpu.VMEM((seq, 1), F32)]
            + _decode_scratch(group, page, heads, dh),
        ),
        out_shape=[jax.ShapeDtypeStruct(q.shape, BF16), jax.ShapeDtypeStruct((b_d, heads, dh), F32)],
        compiler_params=_params(("arbitrary", "arbitrary", "arbitrary")),
        name="sb_attention_fused",
    )(page_table.reshape(-1), bias, q, k, v, q_d, bias_lanes, *([k_pool] * group), *([v_pool] * group))


def _can_fuse_attention(batch, heads, dec_batch, n_pages):
    n_half_steps = 2 * batch * heads
    if n_half_steps % dec_batch:
        return False
    steps_per_seq = n_half_steps // dec_batch
    return steps_per_seq % 2 == 0 and n_pages % steps_per_seq == 0 and n_pages // steps_per_seq <= 8


def _group_norm_gate(o, gate, gn):
    mu = jnp.mean(o, axis=-1, keepdims=True)
    var = jnp.mean((o - mu) ** 2, axis=-1, keepdims=True)
    return (o - mu) * lax.rsqrt(var + EPS) * gn * (gate * jax.nn.sigmoid(gate))


def _ret_prompt_kernel(lg_ref, q_ref, k_ref, v_ref, g_ref, gn_ref, o_ref, s_ref, state, *, chunk, n_chunks):
    lg = lg_ref[pl.program_id(1)]
    n = lax.broadcasted_iota(jnp.int32, (chunk, chunk), 0)
    mcol = lax.broadcasted_iota(jnp.int32, (chunk, chunk), 1)
    diff = (n - mcol).astype(F32)
    decay = jnp.where(diff >= 0, jnp.exp(lg * jnp.maximum(diff, 0.0)), 0.0)
    idx = lax.broadcasted_iota(jnp.int32, (chunk, 1), 0).astype(F32)
    xi = jnp.exp(lg * (idx + 1.0))
    zeta = jnp.exp(lg * (chunk - 1.0 - idx))
    chunk_decay = jnp.exp(lg * jnp.full((1, state.shape[1]), float(chunk), F32))
    gn = gn_ref[...]

    state[...] = jnp.zeros_like(state)

    for c in range(n_chunks):
        rows = slice(c * chunk, (c + 1) * chunk)
        qc, kc, vc = q_ref[rows, :], k_ref[rows, :], v_ref[rows, :]
        s0 = state[...]
        scores = lax.dot_general(qc, kc, (((1,), (1,)), ((), ())), preferred_element_type=F32) * decay
        inner = jnp.dot(scores.astype(BF16), vc, preferred_element_type=F32)
        cross = jnp.dot(qc, s0.astype(BF16), preferred_element_type=F32) * xi
        kz = (kc.astype(F32) * zeta).astype(BF16)
        state[...] = chunk_decay * s0 + lax.dot_general(kz, vc, (((0,), (0,)), ((), ())),
                                                        preferred_element_type=F32)
        o_ref[rows, :] = _group_norm_gate(inner + cross, g_ref[rows, :], gn).astype(o_ref.dtype)
    s_ref[...] = state[...]


def retention_prompt(q, k, v, g, gn, log_gamma, batch, seq):
    dk, dv = RET_QK_DIM, RET_V_DIM
    qk_spec = pl.BlockSpec((seq, dk), lambda b, h: (b, h))
    v_spec = pl.BlockSpec((seq, dv), lambda b, h: (b, h))
    return pl.pallas_call(
        functools.partial(_ret_prompt_kernel, chunk=RET_CHUNK, n_chunks=seq // RET_CHUNK),
        grid=(batch, RET_HEADS),
        in_specs=[pl.BlockSpec(memory_space=pltpu.SMEM), qk_spec, qk_spec, v_spec, v_spec,
                  pl.BlockSpec((1, dv), lambda b, h: (0, h))],
        out_specs=[v_spec, pl.BlockSpec((None, None, dk, dv), lambda b, h: (b, h, 0, 0))],
        out_shape=[jax.ShapeDtypeStruct(v.shape, BF16),
                   jax.ShapeDtypeStruct((batch, RET_HEADS, dk, dv), F32)],
        scratch_shapes=[pltpu.VMEM((dk, dv), F32)],
        compiler_params=_params(("arbitrary", "arbitrary")),
        name="retention_prompt",
    )(log_gamma, q, k, v, g, gn.reshape(1, -1))


def _ret_decode_kernel(lg_ref, q_ref, k_ref, v_ref, g_ref, gn_ref, s0_ref, o_ref, s_ref):
    heads, dk = q_ref.shape
    dv = v_ref.shape[-1]
    pad = 16
    first = lax.broadcasted_iota(jnp.int32, (pad, 1), 0) == 0
    for h in range(heads):
        gamma = jnp.exp(lg_ref[h] * jnp.ones((1, dv), F32))
        q, k, v = q_ref[h:h + 1, :], k_ref[h:h + 1, :], v_ref[h:h + 1, :]
        s0 = s0_ref[h]
        q_rows = jnp.where(first, jnp.broadcast_to(q, (pad, dk)), 0.0).astype(BF16)
        k_rows = jnp.where(first, jnp.broadcast_to(k, (pad, dk)), 0.0).astype(BF16)
        v_rows = jnp.where(first, jnp.broadcast_to(v, (pad, dv)), 0.0).astype(BF16)
        cross = jnp.dot(q_rows, s0.astype(BF16), preferred_element_type=F32)[0:1, :] * gamma
        outer = lax.dot_general(k_rows, v_rows, (((0,), (0,)), ((), ())), preferred_element_type=F32)
        o = jnp.sum(q * k, axis=1, keepdims=True) * v + cross
        s_ref[h] = gamma * s0 + outer
        o_ref[h:h + 1, :] = _group_norm_gate(o, g_ref[h:h + 1, :], gn_ref[h:h + 1, :]).astype(o_ref.dtype)


def retention_decode(q, k, v, g, gn, log_gamma, state):
    b = q.shape[0]
    h, dk, dv = RET_HEADS, RET_QK_DIM, RET_V_DIM
    qk_spec = pl.BlockSpec((None, h, dk), lambda bi: (bi, 0, 0))
    v_spec = pl.BlockSpec((None, h, dv), lambda bi: (bi, 0, 0))
    st_spec = pl.BlockSpec((None, h, dk, dv), lambda bi: (bi, 0, 0, 0))
    o, s_new = pl.pallas_call(
        _ret_decode_kernel,
        grid=(b,),
        in_specs=[pl.BlockSpec(memory_space=pltpu.SMEM), qk_spec, qk_spec, v_spec, v_spec,
                  pl.BlockSpec((h, dv), lambda bi: (0, 0)), st_spec],
        out_specs=[v_spec, st_spec],
        out_shape=[jax.ShapeDtypeStruct((b, h, dv), BF16), jax.ShapeDtypeStruct(state.shape, F32)],
        compiler_params=_params(("arbitrary",)),
        name="retention_decode",
    )(log_gamma, q.reshape(b, h, dk), k.reshape(b, h, dk), v.reshape(b, h, dv), g.reshape(b, h, dv),
      gn.reshape(h, dv), state)
    return o.reshape(b, h * dv), s_new


def _stack(xs):
    return xs[0][None] if len(xs) == 1 else jnp.stack(xs)


def _rope_tables(pos, rows):
    half = RET_QK_DIM // 2
    inv = ROPE_BASE ** (-jnp.arange(half, dtype=F32) / half)
    ang = pos.astype(F32)[:, None] * inv[None, :]
    cos, sin = jnp.cos(ang), jnp.sin(ang)
    reps = rows // pos.shape[0]
    return jnp.tile(cos, (reps, 1)), jnp.tile(sin, (reps, 1))


SAMPLE_ROWS = 16


def kernel(x_prompt, x_sample, cache_k, cache_v, state_ret, page_table, norm_g, ffn_w_gate, ffn_w_up, ffn_w_down,
           sb_w_qkv, sb_w_o, sb_bias, ret_w_qkvg, ret_gn_g, ret_w_o):
    batch, seq, d = x_prompt.shape
    dec_batch, dec_seq, _ = x_sample.shape
    depth = norm_g.shape[0]
    past_len = page_table.shape[1] * PAGE_SIZE
    assert dec_seq == 1, "the decode kernels take one new token per sequence"
    mp, ms = batch * seq, dec_batch * dec_seq
    assert ms <= SAMPLE_ROWS

    def pad_rows(t):
        return jnp.concatenate([t, jnp.zeros((SAMPLE_ROWS - ms, t.shape[1]), t.dtype)], axis=0)

    log_gamma = jnp.log1p(-jnp.exp2(-5.0 - jnp.arange(RET_HEADS, dtype=F32)))
    rope_p = _rope_tables(jnp.arange(seq, dtype=jnp.int32), mp)
    rope_s = _rope_tables(past_len + jnp.arange(dec_seq, dtype=jnp.int32), ms)
    rope_s = (pad_rows(rope_s[0]), pad_rows(rope_s[1]))

    xp, xs = x_prompt.reshape(mp, d), pad_rows(x_sample.reshape(ms, d))
    hp, hs = prenorm(xp, norm_g[0, 0]), prenorm(xs, norm_g[0, 0])
    kp, vp, ksm, vsm, rsp, rss = [], [], [], [], [], []

    for i in range(depth):
        g = norm_g[i]
        layer = i // 2

        def ffn(x, x_s, h, h_s, j, g_post, g_next):
            act, act_s, w_down = swiglu_up(h, h_s, ffn_w_gate, ffn_w_up, (i, j), w_copy_src=ffn_w_down)
            return down_residual(act, act_s, w_down[None, None], (0, 0), x, x_s, g_post, g_next, 0.5)

        xp, xs, hp, hs = ffn(xp, xs, hp, hs, 0, g[1], g[2])

        if i % 2 == 0:
            w = sb_w_qkv[layer]
            bias = sb_bias[layer]
            (q, k, v), (q_s, k_s, v_s), w_o = project(
                hp, hs, w, [(0, BF16, None), (d, F32, None), (2 * d, F32, None)], d, QKV_COL_TILE,
                w_copy_src=sb_w_o[layer])
            q_d = q_s[:ms].reshape(ms, SB_HEADS, SB_HEAD_DIM)
            if _can_fuse_attention(batch, SB_HEADS, ms, page_table.shape[1]):
                mix_p, mix_s = sb_attention_fused(q, k, v, bias, batch, seq, q_d, cache_k, cache_v, layer, page_table)
            else:
                mix_p = sb_attention_prompt(q, k, v, bias, batch, seq)
                mix_s = sb_attention_decode(q_d, cache_k, cache_v, layer, page_table, bias)
            mix_s = mix_s.reshape(ms, d)
            kp.append(k.reshape(batch, seq, SB_HEADS, SB_HEAD_DIM))
            vp.append(v.reshape(batch, seq, SB_HEADS, SB_HEAD_DIM))
            ksm.append(k_s[:ms].reshape(dec_batch, dec_seq, SB_HEADS, SB_HEAD_DIM))
            vsm.append(v_s[:ms].reshape(dec_batch, dec_seq, SB_HEADS, SB_HEAD_DIM))
        else:
            w = ret_w_qkvg[layer]
            gn = ret_gn_g[layer]
            (q, k), (q_s, k_s), _ = project(hp, hs, w, [(0, BF16, RET_QK_DIM ** -0.5), (d, BF16, 1.0)], d, COL_TILE,
                                            rope=rope_p, rope_s=rope_s)
            (v, gate), (v_s, gate_s), w_o = project(hp, hs, w, [(2 * d, BF16, None), (4 * d, F32, None)], 2 * d,
                                                    COL_TILE, w_copy_src=ret_w_o[layer])
            mix_p, st_p = retention_prompt(q, k, v, gate, gn, log_gamma, batch, seq)
            rsp.append(st_p)
            mix_s, st_s = retention_decode(q_s[:ms], k_s[:ms], v_s[:ms], gate_s[:ms], gn, log_gamma,
                                           state_ret[layer])
            rss.append(st_s)

        xp, xs, hp, hs = down_residual(mix_p, pad_rows(mix_s), w_o[None, None], (0, 0), xp, xs, g[3], g[4], 1.0)

        g_next = norm_g[i + 1, 0] if i + 1 < depth else None
        xp, xs, hp, hs = ffn(xp, xs, hp, hs, 1, g[5], g_next)

    return (xp.reshape(batch, seq, d), xs[:ms].reshape(dec_batch, dec_seq, d), _stack(kp), _stack(vp),
            _stack(ksm), _stack(vsm), _stack(rsp), _stack(rss))
```

```python
import functools

import jax
import jax.numpy as jnp
from jax import lax
from jax.experimental import pallas as pl
from jax.experimental.pallas import tpu as pltpu

F32 = jnp.float32
BF16 = jnp.bfloat16

EPS = 1e-6
SB_HEADS = 16
SB_HEAD_DIM = 128
PAGE_SIZE = 128
RET_HEADS = 8
RET_QK_DIM = 256
RET_V_DIM = 512
RET_CHUNK = 128
ROPE_BASE = 10000.0

VMEM_LIMIT_BYTES = 60 * 1024 * 1024
LANES = 128

ROW_TILE = 1024
DOWN_ROW_TILE = 1024
DOWN_ROW_SPLITS = 2
COL_TILE = 512
QKV_COL_TILE = 256
MAX_K_TILE = 1408
ATT_BLOCK = 256
DECODE_PAGES_PER_STEP = 4


def _params(sem):
    return pltpu.CompilerParams(dimension_semantics=sem, vmem_limit_bytes=VMEM_LIMIT_BYTES)


def _rms(x, g):
    return x * lax.rsqrt(jnp.mean(x * x, axis=-1, keepdims=True) + EPS) * g


def _row_tile(m, want):
    return want if m % want == 0 else m


def _prenorm_kernel(x_ref, g_ref, o_ref):
    o_ref[...] = _rms(x_ref[...], g_ref[...]).astype(o_ref.dtype)


def prenorm(x, g):
    m, d = x.shape
    tm = _row_tile(m, DOWN_ROW_TILE)
    return pl.pallas_call(
        _prenorm_kernel,
        grid=(m // tm,),
        in_specs=[pl.BlockSpec((tm, d), lambda i: (i, 0)), pl.BlockSpec((1, d), lambda i: (0, 0))],
        out_specs=pl.BlockSpec((tm, d), lambda i: (i, 0)),
        out_shape=jax.ShapeDtypeStruct((m, d), BF16),
        compiler_params=_params(("arbitrary",)),
        name="prenorm",
    )(x, g.reshape(1, d))


def _swiglu_kernel(*refs, has_copy):
    if has_copy:
        a_ref, as_ref, wg_ref, wu_ref, wsrc_ref, o_ref, os_ref, wcopy_ref, wg_bf, wu_bf = refs
    else:
        a_ref, as_ref, wg_ref, wu_ref, o_ref, os_ref, wg_bf, wu_bf = refs
    first = pl.program_id(1) == 0
    tm = a_ref.shape[0]

    def act(a):
        gate = jnp.dot(a, wg_bf[...], preferred_element_type=F32)
        up = jnp.dot(a, wu_bf[...], preferred_element_type=F32)
        return (gate * jax.nn.sigmoid(gate) * up).astype(o_ref.dtype)

    @pl.when(first)
    def _():
        wg_bf[...] = wg_ref[...].astype(BF16)
        wu_bf[...] = wu_ref[...].astype(BF16)
        y = act(jnp.concatenate([a_ref[...], as_ref[...]], axis=0))
        o_ref[...] = y[:tm]
        os_ref[...] = y[tm:]

    @pl.when(jnp.logical_not(first))
    def _():
        o_ref[...] = act(a_ref[...])

    if has_copy:
        wcopy_ref[...] = wsrc_ref[...].astype(BF16)


def swiglu_up(h, hs, w_gate, w_up, widx, w_copy_src=None):
    m, k = h.shape
    n = w_gate.shape[-1]
    tm, tn = _row_tile(m, ROW_TILE), COL_TILE
    i0, i1 = widx
    wspec = pl.BlockSpec((None, None, k, tn), lambda j, i: (i0, i1, 0, j))
    in_specs = [pl.BlockSpec((tm, k), lambda j, i: (i, 0)), pl.BlockSpec(hs.shape, lambda j, i: (0, 0)), wspec, wspec]
    out_specs = [pl.BlockSpec((tm, tn), lambda j, i: (i, j)), pl.BlockSpec((hs.shape[0], tn), lambda j, i: (0, j))]
    out_shape = [jax.ShapeDtypeStruct((m, n), BF16), jax.ShapeDtypeStruct((hs.shape[0], n), BF16)]
    args = [h, hs, w_gate, w_up]
    if w_copy_src is not None:
        r, d = w_copy_src.shape[-2:]
        n_row = m // tm
        slab = r // ((n // tn) * n_row)
        in_specs.append(pl.BlockSpec((None, None, slab, d), lambda j, i: (i0, i1, j * n_row + i, 0)))
        out_specs.append(pl.BlockSpec((slab, d), lambda j, i: (j * n_row + i, 0)))
        out_shape.append(jax.ShapeDtypeStruct((r, d), BF16))
        args.append(w_copy_src)
    return pl.pallas_call(
        functools.partial(_swiglu_kernel, has_copy=w_copy_src is not None),
        grid=(n // tn, m // tm),
        in_specs=in_specs,
        out_specs=out_specs,
        out_shape=out_shape,
        scratch_shapes=[pltpu.VMEM((k, tn), BF16), pltpu.VMEM((k, tn), BF16)],
        compiler_params=_params(("arbitrary", "arbitrary")),
        name="swiglu_up",
    )(*args)


def _proj_kernel(*refs, n_out, ropes, has_copy):
    has_rope = any(r is not None for r in ropes)
    refs = list(refs)
    a_ref, as_ref = refs[:2]
    w_refs = refs[2:2 + n_out]
    pos = 2 + n_out
    if has_rope:
        cos_ref, sin_ref, cos_s_ref, sin_s_ref = refs[pos:pos + 4]
        pos += 4
    if has_copy:
        wsrc_ref = refs[pos]
        pos += 1
    o_refs = refs[pos:pos + n_out]
    os_refs = refs[pos + n_out:pos + 2 * n_out]
    pos += 2 * n_out
    if has_copy:
        wcopy_ref = refs[pos]
        pos += 1
    w_bfs = refs[pos:]
    first = pl.program_id(1) == 0
    tm = a_ref.shape[0]

    def outputs(a, cos, sin):
        ys = []
        for w_bf, rope_scale in zip(w_bfs, ropes):
            y = jnp.dot(a, w_bf[...], preferred_element_type=F32)
            if rope_scale is not None:
                half = cos.shape[-1]
                parts = []
                for hd in range(y.shape[-1] // (2 * half)):
                    x1 = y[:, hd * 2 * half:hd * 2 * half + half]
                    x2 = y[:, hd * 2 * half + half:(hd + 1) * 2 * half]
                    parts += [(x1 * cos - x2 * sin) * rope_scale, (x1 * sin + x2 * cos) * rope_scale]
                y = jnp.concatenate(parts, axis=1)
            ys.append(y)
        return ys

    @pl.when(first)
    def _():
        for w_ref, w_bf in zip(w_refs, w_bfs):
            w_bf[...] = w_ref[...].astype(BF16)
        cos = jnp.concatenate([cos_ref[...], cos_s_ref[...]], axis=0) if has_rope else None
        sin = jnp.concatenate([sin_ref[...], sin_s_ref[...]], axis=0) if has_rope else None
        ys = outputs(jnp.concatenate([a_ref[...], as_ref[...]], axis=0), cos, sin)
        for y, o_ref, os_ref in zip(ys, o_refs, os_refs):
            o_ref[...] = y[:tm].astype(o_ref.dtype)
            os_ref[...] = y[tm:].astype(os_ref.dtype)

    @pl.when(jnp.logical_not(first))
    def _():
        ys = outputs(a_ref[...], cos_ref[...] if has_rope else None, sin_ref[...] if has_rope else None)
        for y, o_ref in zip(ys, o_refs):
            o_ref[...] = y.astype(o_ref.dtype)

    if has_copy:
        wcopy_ref[...] = wsrc_ref[...].astype(BF16)


def project(h, hs, w, outs, ncols, tn, rope=None, rope_s=None, w_copy_src=None):
    m, k = h.shape
    ms = hs.shape[0]
    tm = _row_tile(m, ROW_TILE)
    ncol_steps = ncols // tn
    ropes = tuple(o[2] for o in outs)
    in_specs = [pl.BlockSpec((tm, k), lambda j, i: (i, 0)), pl.BlockSpec((ms, k), lambda j, i: (0, 0))]
    for c0, _, _ in outs:
        in_specs.append(pl.BlockSpec((k, tn), functools.partial(lambda j, i, off: (0, off + j), off=c0 // tn)))
    args = [h, hs] + [w] * len(outs)
    if rope is not None:
        half = rope[0].shape[-1]
        in_specs += ([pl.BlockSpec((tm, half), lambda j, i: (i, 0))] * 2
                     + [pl.BlockSpec((ms, half), lambda j, i: (0, 0))] * 2)
        args += list(rope) + list(rope_s)
    out_specs = ([pl.BlockSpec((tm, tn), lambda j, i: (i, j))] * len(outs)
                 + [pl.BlockSpec((ms, tn), lambda j, i: (0, j))] * len(outs))
    out_shape = ([jax.ShapeDtypeStruct((m, ncols), o[1]) for o in outs]
                 + [jax.ShapeDtypeStruct((ms, ncols), F32) for _ in outs])
    if w_copy_src is not None:
        r, d = w_copy_src.shape
        n_row = m // tm
        slab = r // (ncol_steps * n_row)
        in_specs.append(pl.BlockSpec((slab, d), lambda j, i: (j * n_row + i, 0)))
        args.append(w_copy_src)
        out_specs.append(pl.BlockSpec((slab, d), lambda j, i: (j * n_row + i, 0)))
        out_shape.append(jax.ShapeDtypeStruct((r, d), BF16))
    res = pl.pallas_call(
        functools.partial(_proj_kernel, n_out=len(outs), ropes=ropes, has_copy=w_copy_src is not None),
        grid=(ncol_steps, m // tm),
        in_specs=in_specs,
        out_specs=out_specs,
        out_shape=out_shape,
        scratch_shapes=[pltpu.VMEM((k, tn), BF16)] * len(outs),
        compiler_params=_params(("arbitrary", "arbitrary")),
        name="project_rope" if rope is not None else "project",
    )(*args)
    n = len(outs)
    return res[:n], res[n:2 * n], (res[2 * n] if w_copy_src is not None else None)


def _down_kernel(*refs, res_scale, nk, n_tiles, n_chunks, has_next, row_splits):
    if has_next:
        (a_ref, as_ref, w_ref, x_ref, xs_ref, gpost_ref, gnext_ref,
         xo_ref, xso_ref, ho_ref, hso_ref, acc, acc_s) = refs
    else:
        a_ref, as_ref, w_ref, x_ref, xs_ref, gpost_ref, xo_ref, xso_ref, acc, acc_s = refs
        gnext_ref = ho_ref = hso_ref = None
    i = pl.program_id(0)
    kk = pl.program_id(1)
    tm = acc.shape[1]
    rows = tm // row_splits
    chunk = tm // n_chunks
    slot = lax.rem(i, 2)
    has_matmul = i < n_tiles

    @pl.when(jnp.logical_and(i == 0, kk == 0))
    def _():
        acc[...] = jnp.zeros_like(acc)
        acc_s[...] = jnp.zeros_like(acc_s)

    @pl.when(jnp.logical_and(jnp.logical_and(i > 0, has_matmul), kk == 0))
    def _():
        acc[slot] = jnp.zeros(acc.shape[1:], acc.dtype)

    def accumulate(with_sample_rows):
        w = w_ref[...]
        for r in range(row_splits):
            sl = slice(r * rows, (r + 1) * rows)
            a = a_ref[sl, :].astype(BF16)
            if with_sample_rows and r == row_splits - 1:
                d = jnp.dot(jnp.concatenate([a, as_ref[...].astype(BF16)], axis=0), w, preferred_element_type=F32)
                acc[slot, sl, :] += d[:rows]
                acc_s[...] += d[rows:]
            else:
                acc[slot, sl, :] += jnp.dot(a, w, preferred_element_type=F32)

    def finish(x, total, o_ref, h_ref):
        xn = x + res_scale * _rms(total, gpost_ref[...])
        o_ref[...] = xn
        if has_next:
            h_ref[...] = _rms(xn, gnext_ref[...]).astype(h_ref.dtype)

    def finish_chunk():
        r0 = pl.multiple_of(jnp.minimum(kk, n_chunks - 1) * chunk, chunk)
        finish(x_ref[...], acc[1 - slot, pl.ds(r0, chunk), :], xo_ref, ho_ref)

    @pl.when(i == 0)
    def _():
        finish_chunk()
        accumulate(True)

    @pl.when(jnp.logical_and(i > 0, has_matmul))
    def _():
        finish_chunk()
        accumulate(False)

    @pl.when(jnp.logical_not(has_matmul))
    def _():
        finish_chunk()

    @pl.when(jnp.logical_and(i == 1, kk == 0))
    def _():
        finish(xs_ref[...], acc_s[...], xso_ref, hso_ref)


def down_residual(a, a_s, w, widx, x, x_s, g_post, g_next, res_scale):
    m, k = a.shape
    ms = a_s.shape[0]
    d = w.shape[-1]
    tm = _row_tile(m, DOWN_ROW_TILE)
    tk = max(t for t in range(LANES, MAX_K_TILE + 1, LANES) if k % t == 0)
    row_splits = DOWN_ROW_SPLITS if tm % (16 * DOWN_ROW_SPLITS) == 0 else 1
    nk = k // tk
    n_tiles = m // tm
    n_chunks = max(c for c in (1, 2, 4, 8) if c <= nk and tm % (16 * c) == 0)
    chunk = tm // n_chunks
    i0, i1 = widx
    has_next = g_next is not None

    def chunk_map(i, kk):
        return (jnp.where(i == 0, 0, (i - 1) * n_chunks + jnp.minimum(kk, n_chunks - 1)), 0)

    crow = pl.BlockSpec((chunk, d), chunk_map)
    srow = pl.BlockSpec((ms, d), lambda i, kk: (0, 0))
    vec = pl.BlockSpec((1, d), lambda i, kk: (0, 0))
    def kstep(i, kk):
        return jnp.where(i < n_tiles, kk, nk - 1)

    in_specs = [pl.BlockSpec((tm, tk), lambda i, kk: (jnp.minimum(i, n_tiles - 1), kstep(i, kk))),
                pl.BlockSpec((ms, tk), lambda i, kk: (0, kstep(i, kk))),
                pl.BlockSpec((None, None, tk, d), lambda i, kk: (i0, i1, kstep(i, kk), 0)), crow, srow, vec]
    args = [a, a_s, w, x, x_s, g_post.reshape(1, d)]
    out_specs = [crow, srow]
    out_shape = [jax.ShapeDtypeStruct((m, d), F32), jax.ShapeDtypeStruct((ms, d), F32)]
    if has_next:
        in_specs.append(vec)
        args.append(g_next.reshape(1, d))
        out_specs += [crow, srow]
        out_shape += [jax.ShapeDtypeStruct((m, d), BF16), jax.ShapeDtypeStruct((ms, d), BF16)]
    outs = pl.pallas_call(
        functools.partial(_down_kernel, res_scale=res_scale, nk=nk, n_tiles=n_tiles, n_chunks=n_chunks,
                          has_next=has_next, row_splits=row_splits),
        grid=(n_tiles + 1, nk),
        in_specs=in_specs,
        out_specs=out_specs,
        out_shape=out_shape,
        scratch_shapes=[pltpu.VMEM((2, tm, d), F32), pltpu.VMEM((ms, d), F32)],
        compiler_params=_params(("arbitrary", "arbitrary")),
        name="down_residual",
    )(*args)
    return tuple(outs) if has_next else (outs[0], outs[1], None, None)


def _softplus(z):
    return jnp.maximum(z, 0.0) + jnp.log(1.0 + jnp.exp(-jnp.abs(z)))


def _suffix_sum(sp, tri):
    hi = sp.astype(BF16)
    lo = (sp - hi.astype(F32)).astype(BF16)
    return jnp.dot(hi, tri, preferred_element_type=F32) + jnp.dot(lo, tri, preferred_element_type=F32)


def _prompt_strips(q_ref, k_bf, v_bf, acc, run, bias, lo, hi, *, blk, scale):
    row = lax.broadcasted_iota(jnp.int32, (blk, blk), 0)
    col = lax.broadcasted_iota(jnp.int32, (blk, blk), 1)
    tri = (row >= col).astype(BF16)
    visible = col < row

    def mask_diagonal(x):
        top = jnp.where(visible, x[:blk], 0.0)
        return top if x.shape[0] == blk else jnp.concatenate([top, x[blk:]], axis=0)

    for j in reversed(range(lo, hi)):
        r0 = j * blk
        z = lax.dot_general(q_ref[r0:, :], k_bf[r0:r0 + blk, :], (((1,), (1,)), ((), ())),
                            preferred_element_type=F32) * scale + bias
        local = _suffix_sum(mask_diagonal(_softplus(z)), tri)
        p = mask_diagonal(jnp.exp(z - local))
        pv = jnp.dot(p.astype(BF16), v_bf[r0:r0 + blk, :], preferred_element_type=F32)
        acc[r0:, :] += jnp.exp(-run[r0:, :]) * pv
        run[r0:, :] += local[:, 0:1]


def _sb_prompt_kernel(bias_ref, q_ref, k_ref, v_ref, o_ref, k_bf, v_bf, acc, run, *, blk, scale):
    k_bf[...] = k_ref[...].astype(BF16)
    v_bf[...] = v_ref[...].astype(BF16)
    acc[...] = jnp.zeros_like(acc)
    run[...] = jnp.zeros_like(run)
    _prompt_strips(q_ref, k_bf, v_bf, acc, run, bias_ref[pl.program_id(1)], 0, q_ref.shape[0] // blk,
                   blk=blk, scale=scale)
    o_ref[...] = acc[...].astype(o_ref.dtype)


def sb_attention_prompt(q, k, v, bias, batch, seq):
    dh = SB_HEAD_DIM
    spec = pl.BlockSpec((seq, dh), lambda b, h: (b, h))
    return pl.pallas_call(
        functools.partial(_sb_prompt_kernel, blk=ATT_BLOCK, scale=dh ** -0.5),
        grid=(batch, SB_HEADS),
        in_specs=[pl.BlockSpec(memory_space=pltpu.SMEM), spec, spec, spec],
        out_specs=spec,
        out_shape=jax.ShapeDtypeStruct(q.shape, BF16),
        scratch_shapes=[pltpu.VMEM((seq, dh), BF16), pltpu.VMEM((seq, dh), BF16),
                        pltpu.VMEM((seq, dh), F32), pltpu.VMEM((seq, 1), F32)],
        compiler_params=_params(("arbitrary", "arbitrary")),
        name="sb_attention_prompt",
    )(bias, q, k, v)


def _decode_pages(q, bias, k_refs, v_refs, acc, run, z_rows, a_flat, g_rows, *, scale):
    page, heads, dh = k_refs[0].shape
    flat = page * heads
    n_groups = flat // LANES
    lane = lax.broadcasted_iota(jnp.int32, (heads, flat), 1)
    own = lax.rem(lane, heads) == lax.broadcasted_iota(jnp.int32, (heads, flat), 0)
    src = lax.broadcasted_iota(jnp.int32, (LANES, 2 * LANES), 0)
    dst = lax.broadcasted_iota(jnp.int32, (LANES, 2 * LANES), 1)
    same_head = lax.rem(src, heads) == lax.rem(dst, heads)
    newer = (src // heads) >= (lax.rem(dst, LANES) // heads)
    sel = (same_head & (newer | (dst >= LANES))).astype(BF16)

    def logits(g):
        kf = k_refs[g][...].reshape(flat, dh).astype(BF16)
        zt = lax.dot_general(q, kf, (((1,), (1,)), ((), ())), preferred_element_type=F32)
        zf = jnp.sum(jnp.where(own, zt, 0.0), axis=0, keepdims=True)
        for c in range(n_groups):
            z_rows[g, c:c + 1, :] = zf[:, c * LANES:(c + 1) * LANES]
        z = z_rows[g] * scale + bias
        both = _suffix_sum(_softplus(z), sel)
        local, group_total = both[:, :LANES], both[:, LANES:]
        newer_groups = jnp.zeros((1, LANES), F32)
        for c in range(n_groups - 1, -1, -1):
            g_rows[g, c:c + 1, :] = newer_groups
            newer_groups = newer_groups + group_total[c:c + 1, :]
        return z - (local + g_rows[g]), newer_groups

    pages = [logits(g) for g in range(len(k_refs))]
    seen = run[...]
    for g, (log_a, page_total) in enumerate(pages):
        a = jnp.exp(log_a - seen)
        seen = seen + page_total
        for c in range(n_groups):
            a_flat[g, :, c * LANES:(c + 1) * LANES] = a[c:c + 1, :]
        a_own = jnp.where(own, jnp.broadcast_to(a_flat[g], (heads, flat)), 0.0).astype(BF16)
        vf = v_refs[g][...].reshape(flat, dh).astype(BF16)
        acc[...] += jnp.dot(a_own, vf, preferred_element_type=F32)
    run[...] = seen


def _decode_scratch(group, page, heads, dh):
    n_groups = page * heads // LANES
    return [pltpu.VMEM((heads, dh), F32), pltpu.VMEM((1, LANES), F32),
            pltpu.VMEM((group, n_groups, LANES), F32), pltpu.VMEM((group, 1, page * heads), F32),
            pltpu.VMEM((group, n_groups, LANES), F32)]


def _sb_decode_kernel(pt_ref, q_ref, bias_ref, *refs, scale, n_steps, group):
    del pt_ref
    k_refs, v_refs = refs[:group], refs[group:2 * group]
    o_ref, acc, run, z_rows, a_flat, g_rows = refs[2 * group:]
    p = pl.program_id(1)

    @pl.when(p == 0)
    def _():
        acc[...] = jnp.zeros_like(acc)
        run[...] = jnp.zeros_like(run)

    _decode_pages(q_ref[...].astype(BF16), bias_ref[...], k_refs, v_refs, acc, run, z_rows, a_flat, g_rows,
                  scale=scale)

    @pl.when(p == n_steps - 1)
    def _():
        o_ref[...] = acc[...]


def sb_attention_decode(q, k_pool, v_pool, layer, page_table, bias):
    b, heads, dh = q.shape
    n_pages = page_table.shape[1]
    page = k_pool.shape[2]
    group = max(g for g in range(1, DECODE_PAGES_PER_STEP + 1) if n_pages % g == 0)
    n_steps = n_pages // group

    def pool_spec(g):
        def page_map(bi, p, pt):
            return (layer, pt[bi * n_pages + (n_pages - 1 - (p * group + g))], 0, 0, 0)
        return pl.BlockSpec((None, None, page, heads, dh), page_map)

    pool_specs = [pool_spec(g) for g in range(group)]
    q_spec = pl.BlockSpec((None, heads, dh), lambda bi, p, pt: (bi, 0, 0))
    bias_lanes = jnp.tile(bias, LANES // heads).reshape(1, LANES)
    return pl.pallas_call(
        functools.partial(_sb_decode_kernel, scale=dh ** -0.5, n_steps=n_steps, group=group),
        grid_spec=pltpu.PrefetchScalarGridSpec(
            num_scalar_prefetch=1,
            grid=(b, n_steps),
            in_specs=[q_spec, pl.BlockSpec((1, LANES), lambda bi, p, pt: (0, 0))] + pool_specs + pool_specs,
            out_specs=q_spec,
            scratch_shapes=_decode_scratch(group, page, heads, dh),
        ),
        out_shape=jax.ShapeDtypeStruct((b, heads, dh), F32),
        compiler_params=_params(("arbitrary", "arbitrary")),
        name="sb_attention_decode",
    )(page_table.reshape(-1), q, bias_lanes, *([k_pool] * group), *([v_pool] * group))


def _sb_fused_kernel(pt_ref, bias_ref, q_ref, k_ref, v_ref, qd_ref, biasd_ref, *refs,
                     blk, scale, group, steps_per_seq):
    del pt_ref
    k_refs, v_refs = refs[:group], refs[group:2 * group]
    o_ref, od_ref, k_bf, v_bf, acc, run, acc_d, run_d, z_rows, a_flat, g_rows = refs[2 * group:]
    h = pl.program_id(1)
    half = pl.program_id(2)
    nblk = q_ref.shape[0] // blk
    step = (pl.program_id(0) * pl.num_programs(1) + h) * 2 + half
    t = lax.rem(step, steps_per_seq)
    bias = bias_ref[h]

    @pl.when(t == 0)
    def _():
        acc_d[...] = jnp.zeros_like(acc_d)
        run_d[...] = jnp.zeros_like(run_d)

    def decode():
        _decode_pages(qd_ref[...].astype(BF16), biasd_ref[...], k_refs, v_refs, acc_d, run_d, z_rows, a_flat,
                      g_rows, scale=scale)

    split = nblk - max(c for c in range(1, nblk + 1) if c * (c + 1) <= nblk * (nblk + 1) // 2)

    @pl.when(half == 0)
    def _():
        k_bf[...] = k_ref[...].astype(BF16)
        v_bf[...] = v_ref[...].astype(BF16)
        acc[...] = jnp.zeros_like(acc)
        run[...] = jnp.zeros_like(run)
        decode()
        _prompt_strips(q_ref, k_bf, v_bf, acc, run, bias, split, nblk, blk=blk, scale=scale)

    @pl.when(half == 1)
    def _():
        decode()
        _prompt_strips(q_ref, k_bf, v_bf, acc, run, bias, 0, split, blk=blk, scale=scale)
        o_ref[...] = acc[...].astype(o_ref.dtype)

    @pl.when(t == steps_per_seq - 1)
    def _():
        od_ref[...] = acc_d[...]


def sb_attention_fused(q, k, v, bias, batch, seq, q_d, k_pool, v_pool, layer, page_table):
    dh = SB_HEAD_DIM
    b_d, heads, _ = q_d.shape
    n_pages = page_table.shape[1]
    page = k_pool.shape[2]
    n_half_steps = 2 * batch * heads
    steps_per_seq = n_half_steps // b_d
    group = n_pages // steps_per_seq

    def pool_spec(g):
        def page_map(b, h, half, pt):
            step = (b * heads + h) * 2 + half
            return (layer, pt[(step // steps_per_seq) * n_pages
                              + (n_pages - 1 - ((step % steps_per_seq) * group + g))], 0, 0, 0)
        return pl.BlockSpec((None, None, page, heads, dh), page_map)

    pool_specs = [pool_spec(g) for g in range(group)]
    spec = pl.BlockSpec((seq, dh), lambda b, h, half, pt: (b, h))
    qd_spec = pl.BlockSpec((None, heads, dh), lambda b, h, half, pt: (((b * heads + h) * 2) // steps_per_seq, 0, 0))
    bias_lanes = jnp.tile(bias, LANES // heads).reshape(1, LANES)
    return pl.pallas_call(
        functools.partial(_sb_fused_kernel, blk=ATT_BLOCK, scale=dh ** -0.5, group=group,
                          steps_per_seq=steps_per_seq),
        grid_spec=pltpu.PrefetchScalarGridSpec(
            num_scalar_prefetch=1,
            grid=(batch, heads, 2),
            in_specs=[pl.BlockSpec(memory_space=pltpu.SMEM), spec, spec, spec, qd_spec,
                      pl.BlockSpec((1, LANES), lambda b, h, half, pt: (0, 0))] + pool_specs + pool_specs,
            out_specs=[spec, qd_spec],
            scratch_shapes=[pltpu.VMEM((seq, dh), BF16), pltpu.VMEM((seq, dh), BF16),
                            pltpu.VMEM((seq, dh), F32), pltpu.VMEM((seq, 1), F32)]
            + _decode_scratch(group, page, heads, dh),
        ),
        out_shape=[jax.ShapeDtypeStruct(q.shape, BF16), jax.ShapeDtypeStruct((b_d, heads, dh), F32)],
        compiler_params=_params(("arbitrary", "arbitrary", "arbitrary")),
        name="sb_attention_fused",
    )(page_table.reshape(-1), bias, q, k, v, q_d, bias_lanes, *([k_pool] * group), *([v_pool] * group))


def _can_fuse_attention(batch, heads, dec_batch, n_pages):
    n_half_steps = 2 * batch * heads
    if n_half_steps % dec_batch:
        return False
    steps_per_seq = n_half_steps // dec_batch
    return steps_per_seq % 2 == 0 and n_pages % steps_per_seq == 0 and n_pages // steps_per_seq <= 8


def _group_norm_gate(o, gate, gn):
    mu = jnp.mean(o, axis=-1, keepdims=True)
    var = jnp.mean((o - mu) ** 2, axis=-1, keepdims=True)
    return (o - mu) * lax.rsqrt(var + EPS) * gn * (gate * jax.nn.sigmoid(gate))


def _ret_prompt_kernel(lg_ref, q_ref, k_ref, v_ref, g_ref, gn_ref, o_ref, s_ref, state, *, chunk, n_chunks):
    lg = lg_ref[pl.program_id(1)]
    n = lax.broadcasted_iota(jnp.int32, (chunk, chunk), 0)
    mcol = lax.broadcasted_iota(jnp.int32, (chunk, chunk), 1)
    diff = (n - mcol).astype(F32)
    decay = jnp.where(diff >= 0, jnp.exp(lg * jnp.maximum(diff, 0.0)), 0.0)
    idx = lax.broadcasted_iota(jnp.int32, (chunk, 1), 0).astype(F32)
    xi = jnp.exp(lg * (idx + 1.0))
    zeta = jnp.exp(lg * (chunk - 1.0 - idx))
    chunk_decay = jnp.exp(lg * jnp.full((1, state.shape[1]), float(chunk), F32))
    gn = gn_ref[...]

    state[...] = jnp.zeros_like(state)

    for c in range(n_chunks):
        rows = slice(c * chunk, (c + 1) * chunk)
        qc, kc, vc = q_ref[rows, :], k_ref[rows, :], v_ref[rows, :]
        s0 = state[...]
        scores = lax.dot_general(qc, kc, (((1,), (1,)), ((), ())), preferred_element_type=F32) * decay
        inner = jnp.dot(scores.astype(BF16), vc, preferred_element_type=F32)
        cross = jnp.dot(qc, s0.astype(BF16), preferred_element_type=F32) * xi
        kz = (kc.astype(F32) * zeta).astype(BF16)
        state[...] = chunk_decay * s0 + lax.dot_general(kz, vc, (((0,), (0,)), ((), ())),
                                                        preferred_element_type=F32)
        o_ref[rows, :] = _group_norm_gate(inner + cross, g_ref[rows, :], gn).astype(o_ref.dtype)
    s_ref[...] = state[...]


def retention_prompt(q, k, v, g, gn, log_gamma, batch, seq):
    dk, dv = RET_QK_DIM, RET_V_DIM
    qk_spec = pl.BlockSpec((seq, dk), lambda b, h: (b, h))
    v_spec = pl.BlockSpec((seq, dv), lambda b, h: (b, h))
    return pl.pallas_call(
        functools.partial(_ret_prompt_kernel, chunk=RET_CHUNK, n_chunks=seq // RET_CHUNK),
        grid=(batch, RET_HEADS),
        in_specs=[pl.BlockSpec(memory_space=pltpu.SMEM), qk_spec, qk_spec, v_spec, v_spec,
                  pl.BlockSpec((1, dv), lambda b, h: (0, h))],
        out_specs=[v_spec, pl.BlockSpec((None, None, dk, dv), lambda b, h: (b, h, 0, 0))],
        out_shape=[jax.ShapeDtypeStruct(v.shape, BF16),
                   jax.ShapeDtypeStruct((batch, RET_HEADS, dk, dv), F32)],
        scratch_shapes=[pltpu.VMEM((dk, dv), F32)],
        compiler_params=_params(("arbitrary", "arbitrary")),
        name="retention_prompt",
    )(log_gamma, q, k, v, g, gn.reshape(1, -1))


def _ret_decode_kernel(lg_ref, q_ref, k_ref, v_ref, g_ref, gn_ref, s0_ref, o_ref, s_ref):
    heads, dk = q_ref.shape
    dv = v_ref.shape[-1]
    pad = 16
    first = lax.broadcasted_iota(jnp.int32, (pad, 1), 0) == 0
    for h in range(heads):
        gamma = jnp.exp(lg_ref[h] * jnp.ones((1, dv), F32))
        q, k, v = q_ref[h:h + 1, :], k_ref[h:h + 1, :], v_ref[h:h + 1, :]
        s0 = s0_ref[h]
        q_rows = jnp.where(first, jnp.broadcast_to(q, (pad, dk)), 0.0).astype(BF16)
        k_rows = jnp.where(first, jnp.broadcast_to(k, (pad, dk)), 0.0).astype(BF16)
        v_rows = jnp.where(first, jnp.broadcast_to(v, (pad, dv)), 0.0).astype(BF16)
        cross = jnp.dot(q_rows, s0.astype(BF16), preferred_element_type=F32)[0:1, :] * gamma
        outer = lax.dot_general(k_rows, v_rows, (((0,), (0,)), ((), ())), preferred_element_type=F32)
        o = jnp.sum(q * k, axis=1, keepdims=True) * v + cross
        s_ref[h] = gamma * s0 + outer
        o_ref[h:h + 1, :] = _group_norm_gate(o, g_ref[h:h + 1, :], gn_ref[h:h + 1, :]).astype(o_ref.dtype)


def retention_decode(q, k, v, g, gn, log_gamma, state):
    b = q.shape[0]
    h, dk, dv = RET_HEADS, RET_QK_DIM, RET_V_DIM
    qk_spec = pl.BlockSpec((None, h, dk), lambda bi: (bi, 0, 0))
    v_spec = pl.BlockSpec((None, h, dv), lambda bi: (bi, 0, 0))
    st_spec = pl.BlockSpec((None, h, dk, dv), lambda bi: (bi, 0, 0, 0))
    o, s_new = pl.pallas_call(
        _ret_decode_kernel,
        grid=(b,),
        in_specs=[pl.BlockSpec(memory_space=pltpu.SMEM), qk_spec, qk_spec, v_spec, v_spec,
                  pl.BlockSpec((h, dv), lambda bi: (0, 0)), st_spec],
        out_specs=[v_spec, st_spec],
        out_shape=[jax.ShapeDtypeStruct((b, h, dv), BF16), jax.ShapeDtypeStruct(state.shape, F32)],
        compiler_params=_params(("arbitrary",)),
        name="retention_decode",
    )(log_gamma, q.reshape(b, h, dk), k.reshape(b, h, dk), v.reshape(b, h, dv), g.reshape(b, h, dv),
      gn.reshape(h, dv), state)
    return o.reshape(b, h * dv), s_new


def _stack(xs):
    return xs[0][None] if len(xs) == 1 else jnp.stack(xs)


def _rope_tables(pos, rows):
    half = RET_QK_DIM // 2
    inv = ROPE_BASE ** (-jnp.arange(half, dtype=F32) / half)
    ang = pos.astype(F32)[:, None] * inv[None, :]
    cos, sin = jnp.cos(ang), jnp.sin(ang)
    reps = rows // pos.shape[0]
    return jnp.tile(cos, (reps, 1)), jnp.tile(sin, (reps, 1))


SAMPLE_ROWS = 16


def kernel(x_prompt, x_sample, cache_k, cache_v, state_ret, page_table, norm_g, ffn_w_gate, ffn_w_up, ffn_w_down,
           sb_w_qkv, sb_w_o, sb_bias, ret_w_qkvg, ret_gn_g, ret_w_o):
    batch, seq, d = x_prompt.shape
    dec_batch, dec_seq, _ = x_sample.shape
    depth = norm_g.shape[0]
    past_len = page_table.shape[1] * PAGE_SIZE
    assert dec_seq == 1, "the decode kernels take one new token per sequence"
    mp, ms = batch * seq, dec_batch * dec_seq
    assert ms <= SAMPLE_ROWS

    def pad_rows(t):
        return jnp.concatenate([t, jnp.zeros((SAMPLE_ROWS - ms, t.shape[1]), t.dtype)], axis=0)

    log_gamma = jnp.log1p(-jnp.exp2(-5.0 - jnp.arange(RET_HEADS, dtype=F32)))
    rope_p = _rope_tables(jnp.arange(seq, dtype=jnp.int32), mp)
    rope_s = _rope_tables(past_len + jnp.arange(dec_seq, dtype=jnp.int32), ms)
    rope_s = (pad_rows(rope_s[0]), pad_rows(rope_s[1]))

    xp, xs = x_prompt.reshape(mp, d), pad_rows(x_sample.reshape(ms, d))
    hp, hs = prenorm(xp, norm_g[0, 0]), prenorm(xs, norm_g[0, 0])
    kp, vp, ksm, vsm, rsp, rss = [], [], [], [], [], []

    for i in range(depth):
        g = norm_g[i]
        layer = i // 2

        def ffn(x, x_s, h, h_s, j, g_post, g_next):
            act, act_s, w_down = swiglu_up(h, h_s, ffn_w_gate, ffn_w_up, (i, j), w_copy_src=ffn_w_down)
            return down_residual(act, act_s, w_down[None, None], (0, 0), x, x_s, g_post, g_next, 0.5)

        xp, xs, hp, hs = ffn(xp, xs, hp, hs, 0, g[1], g[2])

        if i % 2 == 0:
            w = sb_w_qkv[layer]
            bias = sb_bias[layer]
            (q, k, v), (q_s, k_s, v_s), w_o = project(
                hp, hs, w, [(0, BF16, None), (d, F32, None), (2 * d, F32, None)], d, QKV_COL_TILE,
                w_copy_src=sb_w_o[layer])
            q_d = q_s[:ms].reshape(ms, SB_HEADS, SB_HEAD_DIM)
            if _can_fuse_attention(batch, SB_HEADS, ms, page_table.shape[1]):
                mix_p, mix_s = sb_attention_fused(q, k, v, bias, batch, seq, q_d, cache_k, cache_v, layer, page_table)
            else:
                mix_p = sb_attention_prompt(q, k, v, bias, batch, seq)
                mix_s = sb_attention_decode(q_d, cache_k, cache_v, layer, page_table, bias)
            mix_s = mix_s.reshape(ms, d)
            kp.append(k.reshape(batch, seq, SB_HEADS, SB_HEAD_DIM))
            vp.append(v.reshape(batch, seq, SB_HEADS, SB_HEAD_DIM))
            ksm.append(k_s[:ms].reshape(dec_batch, dec_seq, SB_HEADS, SB_HEAD_DIM))
            vsm.append(v_s[:ms].reshape(dec_batch, dec_seq, SB_HEADS, SB_HEAD_DIM))
        else:
            w = ret_w_qkvg[layer]
            gn = ret_gn_g[layer]
            (q, k), (q_s, k_s), _ = project(hp, hs, w, [(0, BF16, RET_QK_DIM ** -0.5), (d, BF16, 1.0)], d, COL_TILE,
                                            rope=rope_p, rope_s=rope_s)
            (v, gate), (v_s, gate_s), w_o = project(hp, hs, w, [(2 * d, BF16, None), (4 * d, F32, None)], 2 * d,
                                                    COL_TILE, w_copy_src=ret_w_o[layer])
            mix_p, st_p = retention_prompt(q, k, v, gate, gn, log_gamma, batch, seq)
            rsp.append(st_p)
            mix_s, st_s = retention_decode(q_s[:ms], k_s[:ms], v_s[:ms], gate_s[:ms], gn, log_gamma,
                                           state_ret[layer])
            rss.append(st_s)

        xp, xs, hp, hs = down_residual(mix_p, pad_rows(mix_s), w_o[None, None], (0, 0), xp, xs, g[3], g[4], 1.0)

        g_next = norm_g[i + 1, 0] if i + 1 < depth else None
        xp, xs, hp, hs = ffn(xp, xs, hp, hs, 1, g[5], g_next)

    return (xp.reshape(batch, seq, d), xs[:ms].reshape(dec_batch, dec_seq, d), _stack(kp), _stack(vp),
            _stack(ksm), _stack(vsm), _stack(rsp), _stack(rss))
```

```python
import functools

import jax
import jax.numpy as jnp
from jax import lax
from jax.experimental import pallas as pl
from jax.experimental.pallas import tpu as pltpu

F32 = jnp.float32
BF16 = jnp.bfloat16

EPS = 1e-6
SB_HEADS = 16
SB_HEAD_DIM = 128
PAGE_SIZE = 128
RET_HEADS = 8
RET_QK_DIM = 256
RET_V_DIM = 512
RET_CHUNK = 128
ROPE_BASE = 10000.0

VMEM_LIMIT_BYTES = 60 * 1024 * 1024
LANES = 128

ROW_TILE = 1024
SWIGLU_ROW_TILE = 2048
DOWN_ROW_TILE = 1024
DOWN_ROW_SPLITS = 2
COL_TILE = 512
QKV_COL_TILE = 256
MAX_K_TILE = 1408
ATT_BLOCK = 256
DECODE_PAGES_PER_STEP = 4


def _params(sem):
    return pltpu.CompilerParams(dimension_semantics=sem, vmem_limit_bytes=VMEM_LIMIT_BYTES)


def _rms(x, g):
    return x * lax.rsqrt(jnp.mean(x * x, axis=-1, keepdims=True) + EPS) * g


def _row_tile(m, want):
    return want if m % want == 0 else m


def _prenorm_kernel(x_ref, g_ref, o_ref):
    o_ref[...] = _rms(x_ref[...], g_ref[...]).astype(o_ref.dtype)


def prenorm(x, g):
    m, d = x.shape
    tm = _row_tile(m, DOWN_ROW_TILE)
    return pl.pallas_call(
        _prenorm_kernel,
        grid=(m // tm,),
        in_specs=[pl.BlockSpec((tm, d), lambda i: (i, 0)), pl.BlockSpec((1, d), lambda i: (0, 0))],
        out_specs=pl.BlockSpec((tm, d), lambda i: (i, 0)),
        out_shape=jax.ShapeDtypeStruct((m, d), BF16),
        compiler_params=_params(("arbitrary",)),
        name="prenorm",
    )(x, g.reshape(1, d))


def _swiglu_kernel(*refs, has_copy):
    if has_copy:
        a_ref, as_ref, wg_ref, wu_ref, wsrc_ref, o_ref, os_ref, wcopy_ref, wg_bf, wu_bf = refs
    else:
        a_ref, as_ref, wg_ref, wu_ref, o_ref, os_ref, wg_bf, wu_bf = refs
    first = pl.program_id(1) == 0
    tm = a_ref.shape[0]

    def act(a):
        gate = jnp.dot(a, wg_bf[...], preferred_element_type=F32)
        up = jnp.dot(a, wu_bf[...], preferred_element_type=F32)
        return (gate * jax.nn.sigmoid(gate) * up).astype(o_ref.dtype)

    @pl.when(first)
    def _():
        wg_bf[...] = wg_ref[...].astype(BF16)
        wu_bf[...] = wu_ref[...].astype(BF16)
        y = act(jnp.concatenate([a_ref[...], as_ref[...]], axis=0))
        o_ref[...] = y[:tm]
        os_ref[...] = y[tm:]

    @pl.when(jnp.logical_not(first))
    def _():
        o_ref[...] = act(a_ref[...])

    if has_copy:
        wcopy_ref[...] = wsrc_ref[...].astype(BF16)


def swiglu_up(h, hs, w_gate, w_up, widx, w_copy_src=None):
    m, k = h.shape
    n = w_gate.shape[-1]
    tm, tn = _row_tile(m, SWIGLU_ROW_TILE), COL_TILE
    i0, i1 = widx
    wspec = pl.BlockSpec((None, None, k, tn), lambda j, i: (i0, i1, 0, j))
    in_specs = [pl.BlockSpec((tm, k), lambda j, i: (i, 0)), pl.BlockSpec(hs.shape, lambda j, i: (0, 0)), wspec, wspec]
    out_specs = [pl.BlockSpec((tm, tn), lambda j, i: (i, j)), pl.BlockSpec((hs.shape[0], tn), lambda j, i: (0, j))]
    out_shape = [jax.ShapeDtypeStruct((m, n), BF16), jax.ShapeDtypeStruct((hs.shape[0], n), BF16)]
    args = [h, hs, w_gate, w_up]
    if w_copy_src is not None:
        r, d = w_copy_src.shape[-2:]
        n_row = m // tm
        slab = r // ((n // tn) * n_row)
        in_specs.append(pl.BlockSpec((None, None, slab, d), lambda j, i: (i0, i1, j * n_row + i, 0)))
        out_specs.append(pl.BlockSpec((slab, d), lambda j, i: (j * n_row + i, 0)))
        out_shape.append(jax.ShapeDtypeStruct((r, d), BF16))
        args.append(w_copy_src)
    return pl.pallas_call(
        functools.partial(_swiglu_kernel, has_copy=w_copy_src is not None),
        grid=(n // tn, m // tm),
        in_specs=in_specs,
        out_specs=out_specs,
        out_shape=out_shape,
        scratch_shapes=[pltpu.VMEM((k, tn), BF16), pltpu.VMEM((k, tn), BF16)],
        compiler_params=_params(("arbitrary", "arbitrary")),
        name="swiglu_up",
    )(*args)


def _proj_kernel(*refs, n_out, ropes, has_copy):
    has_rope = any(r is not None for r in ropes)
    refs = list(refs)
    a_ref, as_ref = refs[:2]
    w_refs = refs[2:2 + n_out]
    pos = 2 + n_out
    if has_rope:
        cos_ref, sin_ref, cos_s_ref, sin_s_ref = refs[pos:pos + 4]
        pos += 4
    if has_copy:
        wsrc_ref = refs[pos]
        pos += 1
    o_refs = refs[pos:pos + n_out]
    os_refs = refs[pos + n_out:pos + 2 * n_out]
    pos += 2 * n_out
    if has_copy:
        wcopy_ref = refs[pos]
        pos += 1
    w_bfs = refs[pos:]
    first = pl.program_id(1) == 0
    tm = a_ref.shape[0]

    def outputs(a, cos, sin):
        ys = []
        for w_bf, rope_scale in zip(w_bfs, ropes):
            y = jnp.dot(a, w_bf[...], preferred_element_type=F32)
            if rope_scale is not None:
                half = cos.shape[-1]
                parts = []
                for hd in range(y.shape[-1] // (2 * half)):
                    x1 = y[:, hd * 2 * half:hd * 2 * half + half]
                    x2 = y[:, hd * 2 * half + half:(hd + 1) * 2 * half]
                    parts += [(x1 * cos - x2 * sin) * rope_scale, (x1 * sin + x2 * cos) * rope_scale]
                y = jnp.concatenate(parts, axis=1)
            ys.append(y)
        return ys

    @pl.when(first)
    def _():
        for w_ref, w_bf in zip(w_refs, w_bfs):
            w_bf[...] = w_ref[...].astype(BF16)
        cos = jnp.concatenate([cos_ref[...], cos_s_ref[...]], axis=0) if has_rope else None
        sin = jnp.concatenate([sin_ref[...], sin_s_ref[...]], axis=0) if has_rope else None
        ys = outputs(jnp.concatenate([a_ref[...], as_ref[...]], axis=0), cos, sin)
        for y, o_ref, os_ref in zip(ys, o_refs, os_refs):
            o_ref[...] = y[:tm].astype(o_ref.dtype)
            os_ref[...] = y[tm:].astype(os_ref.dtype)

    @pl.when(jnp.logical_not(first))
    def _():
        ys = outputs(a_ref[...], cos_ref[...] if has_rope else None, sin_ref[...] if has_rope else None)
        for y, o_ref in zip(ys, o_refs):
            o_ref[...] = y.astype(o_ref.dtype)

    if has_copy:
        wcopy_ref[...] = wsrc_ref[...].astype(BF16)


def project(h, hs, w, outs, ncols, tn, rope=None, rope_s=None, w_copy_src=None):
    m, k = h.shape
    ms = hs.shape[0]
    tm = _row_tile(m, ROW_TILE)
    ncol_steps = ncols // tn
    ropes = tuple(o[2] for o in outs)
    in_specs = [pl.BlockSpec((tm, k), lambda j, i: (i, 0)), pl.BlockSpec((ms, k), lambda j, i: (0, 0))]
    for c0, _, _ in outs:
        in_specs.append(pl.BlockSpec((k, tn), functools.partial(lambda j, i, off: (0, off + j), off=c0 // tn)))
    args = [h, hs] + [w] * len(outs)
    if rope is not None:
        half = rope[0].shape[-1]
        in_specs += ([pl.BlockSpec((tm, half), lambda j, i: (i, 0))] * 2
                     + [pl.BlockSpec((ms, half), lambda j, i: (0, 0))] * 2)
        args += list(rope) + list(rope_s)
    out_specs = ([pl.BlockSpec((tm, tn), lambda j, i: (i, j))] * len(outs)
                 + [pl.BlockSpec((ms, tn), lambda j, i: (0, j))] * len(outs))
    out_shape = ([jax.ShapeDtypeStruct((m, ncols), o[1]) for o in outs]
                 + [jax.ShapeDtypeStruct((ms, ncols), F32) for _ in outs])
    if w_copy_src is not None:
        r, d = w_copy_src.shape
        n_row = m // tm
        slab = r // (ncol_steps * n_row)
        in_specs.append(pl.BlockSpec((slab, d), lambda j, i: (j * n_row + i, 0)))
        args.append(w_copy_src)
        out_specs.append(pl.BlockSpec((slab, d), lambda j, i: (j * n_row + i, 0)))
        out_shape.append(jax.ShapeDtypeStruct((r, d), BF16))
    res = pl.pallas_call(
        functools.partial(_proj_kernel, n_out=len(outs), ropes=ropes, has_copy=w_copy_src is not None),
        grid=(ncol_steps, m // tm),
        in_specs=in_specs,
        out_specs=out_specs,
        out_shape=out_shape,
        scratch_shapes=[pltpu.VMEM((k, tn), BF16)] * len(outs),
        compiler_params=_params(("arbitrary", "arbitrary")),
        name="project_rope" if rope is not None else "project",
    )(*args)
    n = len(outs)
    return res[:n], res[n:2 * n], (res[2 * n] if w_copy_src is not None else None)


def _down_kernel(*refs, res_scale, nk, n_tiles, n_chunks, has_next, row_splits):
    if has_next:
        (a_ref, as_ref, w_ref, x_ref, xs_ref, gpost_ref, gnext_ref,
         xo_ref, xso_ref, ho_ref, hso_ref, acc, acc_s) = refs
    else:
        a_ref, as_ref, w_ref, x_ref, xs_ref, gpost_ref, xo_ref, xso_ref, acc, acc_s = refs
        gnext_ref = ho_ref = hso_ref = None
    i = pl.program_id(0)
    kk = pl.program_id(1)
    tm = acc.shape[1]
    rows = tm // row_splits
    chunk = tm // n_chunks
    slot = lax.rem(i, 2)
    has_matmul = i < n_tiles

    @pl.when(jnp.logical_and(i == 0, kk == 0))
    def _():
        acc[...] = jnp.zeros_like(acc)
        acc_s[...] = jnp.zeros_like(acc_s)

    @pl.when(jnp.logical_and(jnp.logical_and(i > 0, has_matmul), kk == 0))
    def _():
        acc[slot] = jnp.zeros(acc.shape[1:], acc.dtype)

    def accumulate(with_sample_rows):
        w = w_ref[...]
        for r in range(row_splits):
            sl = slice(r * rows, (r + 1) * rows)
            a = a_ref[sl, :].astype(BF16)
            if with_sample_rows and r == row_splits - 1:
                d = jnp.dot(jnp.concatenate([a, as_ref[...].astype(BF16)], axis=0), w, preferred_element_type=F32)
                acc[slot, sl, :] += d[:rows]
                acc_s[...] += d[rows:]
            else:
                acc[slot, sl, :] += jnp.dot(a, w, preferred_element_type=F32)

    def finish(x, total, o_ref, h_ref):
        xn = x + res_scale * _rms(total, gpost_ref[...])
        o_ref[...] = xn
        if has_next:
            h_ref[...] = _rms(xn, gnext_ref[...]).astype(h_ref.dtype)

    def finish_chunk():
        r0 = pl.multiple_of(jnp.minimum(kk, n_chunks - 1) * chunk, chunk)
        finish(x_ref[...], acc[1 - slot, pl.ds(r0, chunk), :], xo_ref, ho_ref)

    @pl.when(i == 0)
    def _():
        finish_chunk()
        accumulate(True)

    @pl.when(jnp.logical_and(i > 0, has_matmul))
    def _():
        finish_chunk()
        accumulate(False)

    @pl.when(jnp.logical_not(has_matmul))
    def _():
        finish_chunk()

    @pl.when(jnp.logical_and(i == 1, kk == 0))
    def _():
        finish(xs_ref[...], acc_s[...], xso_ref, hso_ref)


def down_residual(a, a_s, w, widx, x, x_s, g_post, g_next, res_scale):
    m, k = a.shape
    ms = a_s.shape[0]
    d = w.shape[-1]
    tm = _row_tile(m, DOWN_ROW_TILE)
    tk = max(t for t in range(LANES, MAX_K_TILE + 1, LANES) if k % t == 0)
    row_splits = DOWN_ROW_SPLITS if tm % (16 * DOWN_ROW_SPLITS) == 0 else 1
    nk = k // tk
    n_tiles = m // tm
    n_chunks = max(c for c in (1, 2, 4, 8) if c <= nk and tm % (16 * c) == 0)
    chunk = tm // n_chunks
    i0, i1 = widx
    has_next = g_next is not None

    def chunk_map(i, kk):
        return (jnp.where(i == 0, 0, (i - 1) * n_chunks + jnp.minimum(kk, n_chunks - 1)), 0)

    crow = pl.BlockSpec((chunk, d), chunk_map)
    srow = pl.BlockSpec((ms, d), lambda i, kk: (0, 0))
    vec = pl.BlockSpec((1, d), lambda i, kk: (0, 0))
    def kstep(i, kk):
        return jnp.where(i < n_tiles, kk, nk - 1)

    in_specs = [pl.BlockSpec((tm, tk), lambda i, kk: (jnp.minimum(i, n_tiles - 1), kstep(i, kk))),
                pl.BlockSpec((ms, tk), lambda i, kk: (0, kstep(i, kk))),
                pl.BlockSpec((None, None, tk, d), lambda i, kk: (i0, i1, kstep(i, kk), 0)), crow, srow, vec]
    args = [a, a_s, w, x, x_s, g_post.reshape(1, d)]
    out_specs = [crow, srow]
    out_shape = [jax.ShapeDtypeStruct((m, d), F32), jax.ShapeDtypeStruct((ms, d), F32)]
    if has_next:
        in_specs.append(vec)
        args.append(g_next.reshape(1, d))
        out_specs += [crow, srow]
        out_shape += [jax.ShapeDtypeStruct((m, d), BF16), jax.ShapeDtypeStruct((ms, d), BF16)]
    outs = pl.pallas_call(
        functools.partial(_down_kernel, res_scale=res_scale, nk=nk, n_tiles=n_tiles, n_chunks=n_chunks,
                          has_next=has_next, row_splits=row_splits),
        grid=(n_tiles + 1, nk),
        in_specs=in_specs,
        out_specs=out_specs,
        out_shape=out_shape,
        scratch_shapes=[pltpu.VMEM((2, tm, d), F32), pltpu.VMEM((ms, d), F32)],
        compiler_params=_params(("arbitrary", "arbitrary")),
        name="down_residual",
    )(*args)
    return tuple(outs) if has_next else (outs[0], outs[1], None, None)


def _softplus(z):
    return jnp.maximum(z, 0.0) + jnp.log(1.0 + jnp.exp(-jnp.abs(z)))


def _suffix_sum(sp, tri):
    hi = sp.astype(BF16)
    lo = (sp - hi.astype(F32)).astype(BF16)
    return jnp.dot(hi, tri, preferred_element_type=F32) + jnp.dot(lo, tri, preferred_element_type=F32)


def _prompt_strips(q_ref, k_bf, v_bf, acc, run, bias, lo, hi, *, blk, scale):
    row = lax.broadcasted_iota(jnp.int32, (blk, blk), 0)
    col = lax.broadcasted_iota(jnp.int32, (blk, blk), 1)
    tri = (row >= col).astype(BF16)
    visible = col < row

    def mask_diagonal(x):
        top = jnp.where(visible, x[:blk], 0.0)
        return top if x.shape[0] == blk else jnp.concatenate([top, x[blk:]], axis=0)

    for j in reversed(range(lo, hi)):
        r0 = j * blk
        z = lax.dot_general(q_ref[r0:, :], k_bf[r0:r0 + blk, :], (((1,), (1,)), ((), ())),
                            preferred_element_type=F32) * scale + bias
        local = _suffix_sum(mask_diagonal(_softplus(z)), tri)
        p = mask_diagonal(jnp.exp(z - local))
        pv = jnp.dot(p.astype(BF16), v_bf[r0:r0 + blk, :], preferred_element_type=F32)
        acc[r0:, :] += jnp.exp(-run[r0:, :]) * pv
        run[r0:, :] += local[:, 0:1]


def _sb_prompt_kernel(bias_ref, q_ref, k_ref, v_ref, o_ref, k_bf, v_bf, acc, run, *, blk, scale):
    k_bf[...] = k_ref[...].astype(BF16)
    v_bf[...] = v_ref[...].astype(BF16)
    acc[...] = jnp.zeros_like(acc)
    run[...] = jnp.zeros_like(run)
    _prompt_strips(q_ref, k_bf, v_bf, acc, run, bias_ref[pl.program_id(1)], 0, q_ref.shape[0] // blk,
                   blk=blk, scale=scale)
    o_ref[...] = acc[...].astype(o_ref.dtype)


def sb_attention_prompt(q, k, v, bias, batch, seq):
    dh = SB_HEAD_DIM
    spec = pl.BlockSpec((seq, dh), lambda b, h: (b, h))
    return pl.pallas_call(
        functools.partial(_sb_prompt_kernel, blk=ATT_BLOCK, scale=dh ** -0.5),
        grid=(batch, SB_HEADS),
        in_specs=[pl.BlockSpec(memory_space=pltpu.SMEM), spec, spec, spec],
        out_specs=spec,
        out_shape=jax.ShapeDtypeStruct(q.shape, BF16),
        scratch_shapes=[pltpu.VMEM((seq, dh), BF16), pltpu.VMEM((seq, dh), BF16),
                        pltpu.VMEM((seq, dh), F32), pltpu.VMEM((seq, 1), F32)],
        compiler_params=_params(("arbitrary", "arbitrary")),
        name="sb_attention_prompt",
    )(bias, q, k, v)


def _decode_pages(q, bias, k_refs, v_refs, acc, run, z_rows, a_flat, g_rows, *, scale):
    page, heads, dh = k_refs[0].shape
    flat = page * heads
    n_groups = flat // LANES
    lane = lax.broadcasted_iota(jnp.int32, (heads, flat), 1)
    own = lax.rem(lane, heads) == lax.broadcasted_iota(jnp.int32, (heads, flat), 0)
    src = lax.broadcasted_iota(jnp.int32, (LANES, 2 * LANES), 0)
    dst = lax.broadcasted_iota(jnp.int32, (LANES, 2 * LANES), 1)
    same_head = lax.rem(src, heads) == lax.rem(dst, heads)
    newer = (src // heads) >= (lax.rem(dst, LANES) // heads)
    sel = (same_head & (newer | (dst >= LANES))).astype(BF16)

    def logits(g):
        kf = k_refs[g][...].reshape(flat, dh).astype(BF16)
        zt = lax.dot_general(q, kf, (((1,), (1,)), ((), ())), preferred_element_type=F32)
        zf = jnp.sum(jnp.where(own, zt, 0.0), axis=0, keepdims=True)
        for c in range(n_groups):
            z_rows[g, c:c + 1, :] = zf[:, c * LANES:(c + 1) * LANES]
        z = z_rows[g] * scale + bias
        both = _suffix_sum(_softplus(z), sel)
        local, group_total = both[:, :LANES], both[:, LANES:]
        newer_groups = jnp.zeros((1, LANES), F32)
        for c in range(n_groups - 1, -1, -1):
            g_rows[g, c:c + 1, :] = newer_groups
            newer_groups = newer_groups + group_total[c:c + 1, :]
        return z - (local + g_rows[g]), newer_groups

    pages = [logits(g) for g in range(len(k_refs))]
    seen = run[...]
    for g, (log_a, page_total) in enumerate(pages):
        a = jnp.exp(log_a - seen)
        seen = seen + page_total
        for c in range(n_groups):
            a_flat[g, :, c * LANES:(c + 1) * LANES] = a[c:c + 1, :]
        a_own = jnp.where(own, jnp.broadcast_to(a_flat[g], (heads, flat)), 0.0).astype(BF16)
        vf = v_refs[g][...].reshape(flat, dh).astype(BF16)
        acc[...] += jnp.dot(a_own, vf, preferred_element_type=F32)
    run[...] = seen


def _decode_scratch(group, page, heads, dh):
    n_groups = page * heads // LANES
    return [pltpu.VMEM((heads, dh), F32), pltpu.VMEM((1, LANES), F32),
            pltpu.VMEM((group, n_groups, LANES), F32), pltpu.VMEM((group, 1, page * heads), F32),
            pltpu.VMEM((group, n_groups, LANES), F32)]


def _sb_decode_kernel(pt_ref, q_ref, bias_ref, *refs, scale, n_steps, group):
    del pt_ref
    k_refs, v_refs = refs[:group], refs[group:2 * group]
    o_ref, acc, run, z_rows, a_flat, g_rows = refs[2 * group:]
    p = pl.program_id(1)

    @pl.when(p == 0)
    def _():
        acc[...] = jnp.zeros_like(acc)
        run[...] = jnp.zeros_like(run)

    _decode_pages(q_ref[...].astype(BF16), bias_ref[...], k_refs, v_refs, acc, run, z_rows, a_flat, g_rows,
                  scale=scale)

    @pl.when(p == n_steps - 1)
    def _():
        o_ref[...] = acc[...]


def sb_attention_decode(q, k_pool, v_pool, layer, page_table, bias):
    b, heads, dh = q.shape
    n_pages = page_table.shape[1]
    page = k_pool.shape[2]
    group = max(g for g in range(1, DECODE_PAGES_PER_STEP + 1) if n_pages % g == 0)
    n_steps = n_pages // group

    def pool_spec(g):
        def page_map(bi, p, pt):
            return (layer, pt[bi * n_pages + (n_pages - 1 - (p * group + g))], 0, 0, 0)
        return pl.BlockSpec((None, None, page, heads, dh), page_map)

    pool_specs = [pool_spec(g) for g in range(group)]
    q_spec = pl.BlockSpec((None, heads, dh), lambda bi, p, pt: (bi, 0, 0))
    bias_lanes = jnp.tile(bias, LANES // heads).reshape(1, LANES)
    return pl.pallas_call(
        functools.partial(_sb_decode_kernel, scale=dh ** -0.5, n_steps=n_steps, group=group),
        grid_spec=pltpu.PrefetchScalarGridSpec(
            num_scalar_prefetch=1,
            grid=(b, n_steps),
            in_specs=[q_spec, pl.BlockSpec((1, LANES), lambda bi, p, pt: (0, 0))] + pool_specs + pool_specs,
            out_specs=q_spec,
            scratch_shapes=_decode_scratch(group, page, heads, dh),
        ),
        out_shape=jax.ShapeDtypeStruct((b, heads, dh), F32),
        compiler_params=_params(("arbitrary", "arbitrary")),
        name="sb_attention_decode",
    )(page_table.reshape(-1), q, bias_lanes, *([k_pool] * group), *([v_pool] * group))


def _sb_fused_kernel(pt_ref, bias_ref, q_ref, k_ref, v_ref, qd_ref, biasd_ref, *refs,
                     blk, scale, group, steps_per_seq):
    del pt_ref
    k_refs, v_refs = refs[:group], refs[group:2 * group]
    o_ref, od_ref, k_bf, v_bf, acc, run, acc_d, run_d, z_rows, a_flat, g_rows = refs[2 * group:]
    h = pl.program_id(1)
    half = pl.program_id(2)
    nblk = q_ref.shape[0] // blk
    step = (pl.program_id(0) * pl.num_programs(1) + h) * 2 + half
    t = lax.rem(step, steps_per_seq)
    bias = bias_ref[h]

    @pl.when(t == 0)
    def _():
        acc_d[...] = jnp.zeros_like(acc_d)
        run_d[...] = jnp.zeros_like(run_d)

    def decode():
        _decode_pages(qd_ref[...].astype(BF16), biasd_ref[...], k_refs, v_refs, acc_d, run_d, z_rows, a_flat,
                      g_rows, scale=scale)

    split = nblk - max(c for c in range(1, nblk + 1) if c * (c + 1) <= nblk * (nblk + 1) // 2)

    @pl.when(half == 0)
    def _():
        k_bf[...] = k_ref[...].astype(BF16)
        v_bf[...] = v_ref[...].astype(BF16)
        acc[...] = jnp.zeros_like(acc)
        run[...] = jnp.zeros_like(run)
        decode()
        _prompt_strips(q_ref, k_bf, v_bf, acc, run, bias, split, nblk, blk=blk, scale=scale)

    @pl.when(half == 1)
    def _():
        decode()
        _prompt_strips(q_ref, k_bf, v_bf, acc, run, bias, 0, split, blk=blk, scale=scale)
        o_ref[...] = acc[...].astype(o_ref.dtype)

    @pl.when(t == steps_per_seq - 1)
    def _():
        od_ref[...] = acc_d[...]


def sb_attention_fused(q, k, v, bias, batch, seq, q_d, k_pool, v_pool, layer, page_table):
    dh = SB_HEAD_DIM
    b_d, heads, _ = q_d.shape
    n_pages = page_table.shape[1]
    page = k_pool.shape[2]
    n_half_steps = 2 * batch * heads
    steps_per_seq = n_half_steps // b_d
    group = n_pages // steps_per_seq

    def pool_spec(g):
        def page_map(b, h, half, pt):
            step = (b * heads + h) * 2 + half
            return (layer, pt[(step // steps_per_seq) * n_pages
                              + (n_pages - 1 - ((step % steps_per_seq) * group + g))], 0, 0, 0)
        return pl.BlockSpec((None, None, page, heads, dh), page_map)

    pool_specs = [pool_spec(g) for g in range(group)]
    spec = pl.BlockSpec((seq, dh), lambda b, h, half, pt: (b, h))
    qd_spec = pl.BlockSpec((None, heads, dh), lambda b, h, half, pt: (((b * heads + h) * 2) // steps_per_seq, 0, 0))
    bias_lanes = jnp.tile(bias, LANES // heads).reshape(1, LANES)
    return pl.pallas_call(
        functools.partial(_sb_fused_kernel, blk=ATT_BLOCK, scale=dh ** -0.5, group=group,
                          steps_per_seq=steps_per_seq),
        grid_spec=pltpu.PrefetchScalarGridSpec(
            num_scalar_prefetch=1,
            grid=(batch, heads, 2),
            in_specs=[pl.BlockSpec(memory_space=pltpu.SMEM), spec, spec, spec, qd_spec,
                      pl.BlockSpec((1, LANES), lambda b, h, half, pt: (0, 0))] + pool_specs + pool_specs,
            out_specs=[spec, qd_spec],
            scratch_shapes=[pltpu.VMEM((seq, dh), BF16), pltpu.VMEM((seq, dh), BF16),
                            pltpu.VMEM((seq, dh), F32), pltpu.VMEM((seq, 1), F32)]
            + _decode_scratch(group, page, heads, dh),
        ),
        out_shape=[jax.ShapeDtypeStruct(q.shape, BF16), jax.ShapeDtypeStruct((b_d, heads, dh), F32)],
        compiler_params=_params(("arbitrary", "arbitrary", "arbitrary")),
        name="sb_attention_fused",
    )(page_table.reshape(-1), bias, q, k, v, q_d, bias_lanes, *([k_pool] * group), *([v_pool] * group))


def _can_fuse_attention(batch, heads, dec_batch, n_pages):
    n_half_steps = 2 * batch * heads
    if n_half_steps % dec_batch:
        return False
    steps_per_seq = n_half_steps // dec_batch
    return steps_per_seq % 2 == 0 and n_pages % steps_per_seq == 0 and n_pages // steps_per_seq <= 8


def _group_norm_gate(o, gate, gn):
    mu = jnp.mean(o, axis=-1, keepdims=True)
    var = jnp.mean((o - mu) ** 2, axis=-1, keepdims=True)
    return (o - mu) * lax.rsqrt(var + EPS) * gn * (gate * jax.nn.sigmoid(gate))


def _ret_prompt_kernel(lg_ref, q_ref, k_ref, v_ref, g_ref, gn_ref, o_ref, s_ref, state, *, chunk, n_chunks):
    lg = lg_ref[pl.program_id(1)]
    n = lax.broadcasted_iota(jnp.int32, (chunk, chunk), 0)
    mcol = lax.broadcasted_iota(jnp.int32, (chunk, chunk), 1)
    diff = (n - mcol).astype(F32)
    decay = jnp.where(diff >= 0, jnp.exp(lg * jnp.maximum(diff, 0.0)), 0.0)
    idx = lax.broadcasted_iota(jnp.int32, (chunk, 1), 0).astype(F32)
    xi = jnp.exp(lg * (idx + 1.0))
    zeta = jnp.exp(lg * (chunk - 1.0 - idx))
    chunk_decay = jnp.exp(lg * jnp.full((1, state.shape[1]), float(chunk), F32))
    gn = gn_ref[...]

    state[...] = jnp.zeros_like(state)

    for c in range(n_chunks):
        rows = slice(c * chunk, (c + 1) * chunk)
        qc, kc, vc = q_ref[rows, :], k_ref[rows, :], v_ref[rows, :]
        s0 = state[...]
        scores = lax.dot_general(qc, kc, (((1,), (1,)), ((), ())), preferred_element_type=F32) * decay
        inner = jnp.dot(scores.astype(BF16), vc, preferred_element_type=F32)
        cross = jnp.dot(qc, s0.astype(BF16), preferred_element_type=F32) * xi
        kz = (kc.astype(F32) * zeta).astype(BF16)
        state[...] = chunk_decay * s0 + lax.dot_general(kz, vc, (((0,), (0,)), ((), ())),
                                                        preferred_element_type=F32)
        o_ref[rows, :] = _group_norm_gate(inner + cross, g_ref[rows, :], gn).astype(o_ref.dtype)
    s_ref[...] = state[...]


def retention_prompt(q, k, v, g, gn, log_gamma, batch, seq):
    dk, dv = RET_QK_DIM, RET_V_DIM
    qk_spec = pl.BlockSpec((seq, dk), lambda b, h: (b, h))
    v_spec = pl.BlockSpec((seq, dv), lambda b, h: (b, h))
    return pl.pallas_call(
        functools.partial(_ret_prompt_kernel, chunk=RET_CHUNK, n_chunks=seq // RET_CHUNK),
        grid=(batch, RET_HEADS),
        in_specs=[pl.BlockSpec(memory_space=pltpu.SMEM), qk_spec, qk_spec, v_spec, v_spec,
                  pl.BlockSpec((1, dv), lambda b, h: (0, h))],
        out_specs=[v_spec, pl.BlockSpec((None, None, dk, dv), lambda b, h: (b, h, 0, 0))],
        out_shape=[jax.ShapeDtypeStruct(v.shape, BF16),
                   jax.ShapeDtypeStruct((batch, RET_HEADS, dk, dv), F32)],
        scratch_shapes=[pltpu.VMEM((dk, dv), F32)],
        compiler_params=_params(("arbitrary", "arbitrary")),
        name="retention_prompt",
    )(log_gamma, q, k, v, g, gn.reshape(1, -1))


def _ret_decode_kernel(lg_ref, q_ref, k_ref, v_ref, g_ref, gn_ref, s0_ref, o_ref, s_ref):
    heads, dk = q_ref.shape
    dv = v_ref.shape[-1]
    pad = 16
    first = lax.broadcasted_iota(jnp.int32, (pad, 1), 0) == 0
    for h in range(heads):
        gamma = jnp.exp(lg_ref[h] * jnp.ones((1, dv), F32))
        q, k, v = q_ref[h:h + 1, :], k_ref[h:h + 1, :], v_ref[h:h + 1, :]
        s0 = s0_ref[h]
        q_rows = jnp.where(first, jnp.broadcast_to(q, (pad, dk)), 0.0).astype(BF16)
        k_rows = jnp.where(first, jnp.broadcast_to(k, (pad, dk)), 0.0).astype(BF16)
        v_rows = jnp.where(first, jnp.broadcast_to(v, (pad, dv)), 0.0).astype(BF16)
        cross = jnp.dot(q_rows, s0.astype(BF16), preferred_element_type=F32)[0:1, :] * gamma
        outer = lax.dot_general(k_rows, v_rows, (((0,), (0,)), ((), ())), preferred_element_type=F32)
        o = jnp.sum(q * k, axis=1, keepdims=True) * v + cross
        s_ref[h] = gamma * s0 + outer
        o_ref[h:h + 1, :] = _group_norm_gate(o, g_ref[h:h + 1, :], gn_ref[h:h + 1, :]).astype(o_ref.dtype)


def retention_decode(q, k, v, g, gn, log_gamma, state):
    b = q.shape[0]
    h, dk, dv = RET_HEADS, RET_QK_DIM, RET_V_DIM
    qk_spec = pl.BlockSpec((None, h, dk), lambda bi: (bi, 0, 0))
    v_spec = pl.BlockSpec((None, h, dv), lambda bi: (bi, 0, 0))
    st_spec = pl.BlockSpec((None, h, dk, dv), lambda bi: (bi, 0, 0, 0))
    o, s_new = pl.pallas_call(
        _ret_decode_kernel,
        grid=(b,),
        in_specs=[pl.BlockSpec(memory_space=pltpu.SMEM), qk_spec, qk_spec, v_spec, v_spec,
                  pl.BlockSpec((h, dv), lambda bi: (0, 0)), st_spec],
        out_specs=[v_spec, st_spec],
        out_shape=[jax.ShapeDtypeStruct((b, h, dv), BF16), jax.ShapeDtypeStruct(state.shape, F32)],
        compiler_params=_params(("arbitrary",)),
        name="retention_decode",
    )(log_gamma, q.reshape(b, h, dk), k.reshape(b, h, dk), v.reshape(b, h, dv), g.reshape(b, h, dv),
      gn.reshape(h, dv), state)
    return o.reshape(b, h * dv), s_new


def _stack(xs):
    return xs[0][None] if len(xs) == 1 else jnp.stack(xs)


def _rope_tables(pos, rows):
    half = RET_QK_DIM // 2
    inv = ROPE_BASE ** (-jnp.arange(half, dtype=F32) / half)
    ang = pos.astype(F32)[:, None] * inv[None, :]
    cos, sin = jnp.cos(ang), jnp.sin(ang)
    reps = rows // pos.shape[0]
    return jnp.tile(cos, (reps, 1)), jnp.tile(sin, (reps, 1))


SAMPLE_ROWS = 16


def kernel(x_prompt, x_sample, cache_k, cache_v, state_ret, page_table, norm_g, ffn_w_gate, ffn_w_up, ffn_w_down,
           sb_w_qkv, sb_w_o, sb_bias, ret_w_qkvg, ret_gn_g, ret_w_o):
    batch, seq, d = x_prompt.shape
    dec_batch, dec_seq, _ = x_sample.shape
    depth = norm_g.shape[0]
    past_len = page_table.shape[1] * PAGE_SIZE
    assert dec_seq == 1, "the decode kernels take one new token per sequence"
    mp, ms = batch * seq, dec_batch * dec_seq
    assert ms <= SAMPLE_ROWS

    def pad_rows(t):
        return jnp.concatenate([t, jnp.zeros((SAMPLE_ROWS - ms, t.shape[1]), t.dtype)], axis=0)

    log_gamma = jnp.log1p(-jnp.exp2(-5.0 - jnp.arange(RET_HEADS, dtype=F32)))
    rope_p = _rope_tables(jnp.arange(seq, dtype=jnp.int32), mp)
    rope_s = _rope_tables(past_len + jnp.arange(dec_seq, dtype=jnp.int32), ms)
    rope_s = (pad_rows(rope_s[0]), pad_rows(rope_s[1]))

    xp, xs = x_prompt.reshape(mp, d), pad_rows(x_sample.reshape(ms, d))
    hp, hs = prenorm(xp, norm_g[0, 0]), prenorm(xs, norm_g[0, 0])
    kp, vp, ksm, vsm, rsp, rss = [], [], [], [], [], []

    for i in range(depth):
        g = norm_g[i]
        layer = i // 2

        def ffn(x, x_s, h, h_s, j, g_post, g_next):
            act, act_s, w_down = swiglu_up(h, h_s, ffn_w_gate, ffn_w_up, (i, j), w_copy_src=ffn_w_down)
            return down_residual(act, act_s, w_down[None, None], (0, 0), x, x_s, g_post, g_next, 0.5)

        xp, xs, hp, hs = ffn(xp, xs, hp, hs, 0, g[1], g[2])

        if i % 2 == 0:
            w = sb_w_qkv[layer]
            bias = sb_bias[layer]
            (q, k, v), (q_s, k_s, v_s), w_o = project(
                hp, hs, w, [(0, BF16, None), (d, F32, None), (2 * d, F32, None)], d, QKV_COL_TILE,
                w_copy_src=sb_w_o[layer])
            q_d = q_s[:ms].reshape(ms, SB_HEADS, SB_HEAD_DIM)
            if _can_fuse_attention(batch, SB_HEADS, ms, page_table.shape[1]):
                mix_p, mix_s = sb_attention_fused(q, k, v, bias, batch, seq, q_d, cache_k, cache_v, layer, page_table)
            else:
                mix_p = sb_attention_prompt(q, k, v, bias, batch, seq)
                mix_s = sb_attention_decode(q_d, cache_k, cache_v, layer, page_table, bias)
            mix_s = mix_s.reshape(ms, d)
            kp.append(k.reshape(batch, seq, SB_HEADS, SB_HEAD_DIM))
            vp.append(v.reshape(batch, seq, SB_HEADS, SB_HEAD_DIM))
            ksm.append(k_s[:ms].reshape(dec_batch, dec_seq, SB_HEADS, SB_HEAD_DIM))
            vsm.append(v_s[:ms].reshape(dec_batch, dec_seq, SB_HEADS, SB_HEAD_DIM))
        else:
            w = ret_w_qkvg[layer]
            gn = ret_gn_g[layer]
            (q, k), (q_s, k_s), _ = project(hp, hs, w, [(0, BF16, RET_QK_DIM ** -0.5), (d, BF16, 1.0)], d, COL_TILE,
                                            rope=rope_p, rope_s=rope_s)
            (v, gate), (v_s, gate_s), w_o = project(hp, hs, w, [(2 * d, BF16, None), (4 * d, F32, None)], 2 * d,
                                                    COL_TILE, w_copy_src=ret_w_o[layer])
            mix_p, st_p = retention_prompt(q, k, v, gate, gn, log_gamma, batch, seq)
            rsp.append(st_p)
            mix_s, st_s = retention_decode(q_s[:ms], k_s[:ms], v_s[:ms], gate_s[:ms], gn, log_gamma,
                                           state_ret[layer])
            rss.append(st_s)

        xp, xs, hp, hs = down_residual(mix_p, pad_rows(mix_s), w_o[None, None], (0, 0), xp, xs, g[3], g[4], 1.0)

        g_next = norm_g[i + 1, 0] if i + 1 < depth else None
        xp, xs, hp, hs = ffn(xp, xs, hp, hs, 1, g[5], g_next)

    return (xp.reshape(batch, seq, d), xs[:ms].reshape(dec_batch, dec_seq, d), _stack(kp), _stack(vp),
            _stack(ksm), _stack(vsm), _stack(rsp), _stack(rss))
```
